```python
import jax, jax.numpy as jnp
from jax import lax
import numpy as np

D_MODEL = 1024
BATCH = 4
SEQ = 8192
DEPTH = 2
DEC_BATCH = 4
DEC_SEQ = 4096
PAST_LEN = 128

D_MIX = D_MODEL
CONV_WIDTH = D_MIX // 4
CONV_KERNEL = 31
CONV_PAD = CONV_KERNEL // 2
SGU_WIDTH = D_MIX // 4
SGU_HEADS = 4
SGU_HEAD_DIM = SGU_WIDTH // SGU_HEADS
SGU_CHUNK = 128
HGRN_WIDTH = D_MIX // 2
HGRN_EXPAND = 128
HGRN_HEADS = HGRN_WIDTH // HGRN_EXPAND
HGRN_HEAD_V = HGRN_WIDTH // HGRN_HEADS
HGRN_CHUNK = 64
SPLIT_SIZES = (CONV_WIDTH, CONV_WIDTH, CONV_WIDTH,
               SGU_WIDTH, SGU_WIDTH, SGU_WIDTH,
               HGRN_WIDTH, HGRN_WIDTH, HGRN_WIDTH, HGRN_WIDTH, HGRN_WIDTH)
SPLIT_OFFSETS = tuple(int(o) for o in np.cumsum(SPLIT_SIZES)[:-1])
D_IN = int(sum(SPLIT_SIZES))
EPS = 1e-6

kernel_name = "hymba_conv_sgu_hgrn2_bidir_encoder"


def rmsnorm(x, g):
    xf = x.astype(jnp.float32)
    y = xf * lax.rsqrt(jnp.mean(xf * xf, axis=-1, keepdims=True) + EPS)
    return (y * g.astype(jnp.float32)).astype(x.dtype)


def layernorm(x, g, b):
    xf = x.astype(jnp.float32)
    mu = jnp.mean(xf, axis=-1, keepdims=True)
    xc = xf - mu
    y = xc * lax.rsqrt(jnp.mean(xc * xc, axis=-1, keepdims=True) + EPS)
    return (y * g.astype(jnp.float32) + b.astype(jnp.float32)).astype(x.dtype)


def conv_branch(a_val, a_glu, a_gate, conv_w, conv_b, ln_g, ln_b, pw):
    y = a_val * jax.nn.sigmoid(a_glu)
    y = lax.conv_general_dilated(
        y, conv_w[:, None, :], window_strides=(1,), padding=[(CONV_PAD, CONV_PAD)],
        dimension_numbers=('NWC', 'WIO', 'NWC'), feature_group_count=CONV_WIDTH) + conv_b
    y = jax.nn.silu(layernorm(y, ln_g, ln_b))
    y = y @ pw
    return y * jax.nn.silu(a_gate)


def sgu_branch(b_u, b_v, b_gate, ln_g, ln_b, w_s, b_s):
    bsz, seq, _ = b_u.shape
    u = jax.nn.gelu(b_u, approximate=False)
    v = layernorm(jax.nn.gelu(b_v, approximate=False), ln_g, ln_b)
    v = v.reshape(bsz, seq // SGU_CHUNK, SGU_CHUNK, SGU_HEADS, SGU_HEAD_DIM)
    s = jnp.einsum('hts,bnshd->bnthd', w_s, v) + b_s.T[:, :, None]
    s = s.reshape(bsz, seq, SGU_WIDTH)
    return u * s * jax.nn.silu(b_gate)


def hgrn2_scan(q, k, logf, v):
    bsz, seq, nh, dk = q.shape
    dv = v.shape[-1]
    n = seq // HGRN_CHUNK

    def chunks(t):
        return t.reshape(bsz, n, HGRN_CHUNK, nh, t.shape[-1]).transpose(1, 0, 3, 2, 4)

    qc, kc, vc = chunks(q), chunks(k), chunks(v)
    bc = jnp.cumsum(chunks(logf), axis=3)
    mask = jnp.tril(jnp.ones((HGRN_CHUNK, HGRN_CHUNK), dtype=bool))

    def step(state, inp):
        qt, kt, vt, bt = inp
        diff = bt[:, :, :, None, :] - bt[:, :, None, :, :]
        decay = jnp.exp(jnp.where(mask[:, :, None], diff, -jnp.inf))
        scores = jnp.einsum('bhtk,bhsk,bhtsk->bhts', qt, kt, decay)
        o = (jnp.einsum('bhts,bhsv->bhtv', scores, vt)
             + jnp.einsum('bhtk,bhkv->bhtv', qt * jnp.exp(bt), state))
        b_last = bt[:, :, -1:, :]
        state = (jnp.exp(b_last[:, :, 0, :, None]) * state
                 + jnp.einsum('bhsk,bhsv->bhkv', kt * jnp.exp(b_last - bt), vt))
        return state, o

    s0 = jnp.zeros((bsz, nh, dk, dv), jnp.float32)
    _, o = lax.scan(step, s0, (qc, kc, vc, bc))
    return o.transpose(1, 0, 3, 2, 4).reshape(bsz, seq, nh, dv)


def hgrn_branch(c_q, c_i, c_ff, c_fb, c_gate, lb_f, lb_b, norm_g):
    bsz, seq, _ = c_q.shape

    def heads(t):
        return t.astype(jnp.float32).reshape(bsz, seq, HGRN_HEADS, -1)

    q, v = heads(c_q), heads(c_i)

    def forget(z, lb):
        lb = lb.astype(jnp.float32).reshape(HGRN_HEADS, HGRN_EXPAND)
        logf = jnp.logaddexp(jnp.log(lb), jnp.log1p(-lb) + jax.nn.log_sigmoid(z))
        k = (1.0 - lb) * jax.nn.sigmoid(-z)
        return k, logf

    k_f, logf_f = forget(heads(c_ff), lb_f)
    k_b, logf_b = forget(heads(c_fb), lb_b)
    flip = lambda t: jnp.flip(t, axis=1)
    o_f = hgrn2_scan(q, k_f, logf_f, v)
    o_b = flip(hgrn2_scan(flip(q), flip(k_b), flip(logf_b), flip(v)))
    o = rmsnorm(o_f + o_b, norm_g)
    o = o.reshape(bsz, seq, HGRN_WIDTH).astype(c_gate.dtype)
    return o * jax.nn.silu(c_gate)


def trunk(x, norm_g, w_in, conv_w, conv_b, conv_ln_g, conv_ln_b, conv_pw,
          sgu_ln_g, sgu_ln_b, sgu_w, sgu_b, hgrn_lb_fwd, hgrn_lb_bwd, hgrn_norm_g,
          w_out, final_norm_g):
    def lower_bounds(p):
        lb = jnp.cumsum(jax.nn.softmax(p.astype(jnp.float32), axis=0), axis=0)
        return lb - lb[0:1]

    lbf_all = lower_bounds(hgrn_lb_fwd)
    lbb_all = lower_bounds(hgrn_lb_bwd)
    for l in range(DEPTH):
        h = rmsnorm(x, norm_g[l])
        p = h @ w_in[l]
        (a_val, a_glu, a_gate, b_u, b_v, b_gate,
         c_q, c_i, c_ff, c_fb, c_gate) = jnp.split(p, SPLIT_OFFSETS, axis=-1)
        y_a = conv_branch(a_val, a_glu, a_gate, conv_w[l], conv_b[l],
                          conv_ln_g[l], conv_ln_b[l], conv_pw[l])
        y_b = sgu_branch(b_u, b_v, b_gate, sgu_ln_g[l], sgu_ln_b[l], sgu_w[l], sgu_b[l])
        y_c = hgrn_branch(c_q, c_i, c_ff, c_fb, c_gate, lbf_all[l], lbb_all[l], hgrn_norm_g[l])
        y = jnp.concatenate([y_a, y_b, y_c], axis=-1)
        x = x + y @ w_out[l]
    return rmsnorm(x, final_norm_g)


def setup_inputs(seed: int = 0) -> dict:
    key = jax.random.key(seed)
    ks = jax.random.split(key, 20)
    f32 = jnp.float32
    nrm = lambda k, shape, scale: (jax.random.normal(k, shape, f32) * scale).astype(f32)
    return {
        "x_prompt": nrm(ks[0], (BATCH, SEQ, D_MODEL), 1.0),
        "x_sample": nrm(ks[1], (DEC_BATCH, DEC_SEQ, D_MODEL), 1.0),
        "norm_g": 1.0 + nrm(ks[2], (DEPTH, D_MODEL), 0.02),
        "w_in": nrm(ks[3], (DEPTH, D_MODEL, D_IN), D_MODEL ** -0.5),
        "conv_w": nrm(ks[4], (DEPTH, CONV_KERNEL, CONV_WIDTH), CONV_KERNEL ** -0.5),
        "conv_b": nrm(ks[5], (DEPTH, CONV_WIDTH), 0.02),
        "conv_ln_g": 1.0 + nrm(ks[6], (DEPTH, CONV_WIDTH), 0.02),
        "conv_ln_b": nrm(ks[7], (DEPTH, CONV_WIDTH), 0.02),
        "conv_pw": nrm(ks[8], (DEPTH, CONV_WIDTH, CONV_WIDTH), CONV_WIDTH ** -0.5),
        "sgu_ln_g": 1.0 + nrm(ks[9], (DEPTH, SGU_WIDTH), 0.02),
        "sgu_ln_b": nrm(ks[10], (DEPTH, SGU_WIDTH), 0.02),
        "sgu_w": nrm(ks[11], (DEPTH, SGU_HEADS, SGU_CHUNK, SGU_CHUNK), SGU_CHUNK ** -0.5),
        "sgu_b": 1.0 + nrm(ks[12], (DEPTH, SGU_HEADS, SGU_CHUNK), 0.02),
        "hgrn_lb_fwd": nrm(ks[13], (DEPTH, HGRN_WIDTH), 0.5),
        "hgrn_lb_bwd": nrm(ks[14], (DEPTH, HGRN_WIDTH), 0.5),
        "hgrn_norm_g": 1.0 + nrm(ks[15], (DEPTH, HGRN_HEAD_V), 0.02),
        "w_out": nrm(ks[16], (DEPTH, D_MIX, D_MODEL), (2.0 * D_MIX) ** -0.5),
        "final_norm_g": 1.0 + nrm(ks[17], (D_MODEL,), 0.02),
    }


def reference(x_prompt, x_sample, norm_g, w_in, conv_w, conv_b, conv_ln_g, conv_ln_b, conv_pw,
              sgu_ln_g, sgu_ln_b, sgu_w, sgu_b, hgrn_lb_fwd, hgrn_lb_bwd, hgrn_norm_g,
              w_out, final_norm_g):
    y_prompt = trunk(x_prompt, norm_g, w_in, conv_w, conv_b, conv_ln_g, conv_ln_b, conv_pw,
                     sgu_ln_g, sgu_ln_b, sgu_w, sgu_b, hgrn_lb_fwd, hgrn_lb_bwd, hgrn_norm_g,
                     w_out, final_norm_g)
    y_sample = trunk(x_sample, norm_g, w_in, conv_w, conv_b, conv_ln_g, conv_ln_b, conv_pw,
                     sgu_ln_g, sgu_ln_b, sgu_w, sgu_b, hgrn_lb_fwd, hgrn_lb_bwd, hgrn_norm_g,
                     w_out, final_norm_g)
    return (y_prompt, y_sample)
```

```python
import functools

import jax
import jax.numpy as jnp
from jax import lax
from jax.experimental import pallas as pl
from jax.experimental.pallas import tpu as pltpu

F32 = jnp.float32
BF16 = jnp.bfloat16

D_MODEL = 1024
DEPTH = 2
CONV_WIDTH = 256
CONV_KERNEL = 31
CONV_PAD = CONV_KERNEL // 2
SGU_WIDTH = 256
SGU_HEADS = 4
SGU_HEAD_DIM = SGU_WIDTH // SGU_HEADS
SGU_CHUNK = 128
HGRN_WIDTH = 512
HGRN_HEADS = 4
HGRN_DK = 128
D_IN = 4096
D_MIX = 1024
EPS = 1e-6

CONV_COLS = (0, 1, 2)
SGU_COLS = (3, 4, 5)
HGRN_COLS = (3, 4, 5, 6, 7)

P_DTYPE = BF16
Y_DTYPE = BF16

V7X_VMEM_LIMIT_BYTES = 56 * 1024 * 1024

TOKEN_TILE = 512
CONV_HALO = 16
CONV_ROWS = 32
HGRN_CHUNK = 128
HGRN_BLOCK = 16
HGRN_LEVELS = 7


def _params(sem):
    return pltpu.CompilerParams(dimension_semantics=sem, vmem_limit_bytes=V7X_VMEM_LIMIT_BYTES)


def _inproj_kernel(x_ref, g_ref, w_ref, p_ref):
    x = x_ref[...]
    ms = jnp.mean(x * x, axis=-1, keepdims=True)
    h = (x * lax.rsqrt(ms + EPS) * g_ref[...]).astype(BF16)
    tn = 1024
    for j in range(D_IN // tn):
        p_ref[:, j * tn:(j + 1) * tn] = jnp.dot(
            h, w_ref[:, j * tn:(j + 1) * tn], preferred_element_type=F32).astype(p_ref.dtype)


def _inproj(x2d, g, w_bf16):
    t = x2d.shape[0]
    tm = TOKEN_TILE
    return pl.pallas_call(
        _inproj_kernel,
        grid=(t // tm,),
        in_specs=[
            pl.BlockSpec((tm, D_MODEL), lambda i: (i, 0)),
            pl.BlockSpec((1, D_MODEL), lambda i: (0, 0)),
            pl.BlockSpec((D_MODEL, D_IN), lambda i: (0, 0)),
        ],
        out_specs=pl.BlockSpec((tm, D_IN), lambda i: (i, 0)),
        out_shape=jax.ShapeDtypeStruct((t, D_IN), P_DTYPE),
        compiler_params=_params(("parallel",)),
        name="inproj",
    )(x2d, g, w_bf16)


def _outproj_kernel(x_ref, ya_ref, yb_ref, yc_ref, w_ref, g_ref, o_ref, *, final):
    acc = x_ref[...]
    acc = acc + jnp.dot(ya_ref[...], w_ref[0:CONV_WIDTH, :], preferred_element_type=F32)
    acc = acc + jnp.dot(yb_ref[...], w_ref[CONV_WIDTH:CONV_WIDTH + SGU_WIDTH, :],
                        preferred_element_type=F32)
    acc = acc + jnp.dot(yc_ref[...], w_ref[CONV_WIDTH + SGU_WIDTH:, :], preferred_element_type=F32)
    if final:
        ms = jnp.mean(acc * acc, axis=-1, keepdims=True)
        acc = acc * lax.rsqrt(ms + EPS) * g_ref[...]
    o_ref[...] = acc


def _outproj(x2d, ya, yb, yc, w_bf16, g, final):
    t = x2d.shape[0]
    tm = TOKEN_TILE
    return pl.pallas_call(
        functools.partial(_outproj_kernel, final=final),
        grid=(t // tm,),
        in_specs=[
            pl.BlockSpec((tm, D_MODEL), lambda i: (i, 0)),
            pl.BlockSpec((tm, CONV_WIDTH), lambda i: (i, 0)),
            pl.BlockSpec((tm, SGU_WIDTH), lambda i: (i, 0)),
            pl.BlockSpec((tm, HGRN_WIDTH), lambda i: (i, 0)),
            pl.BlockSpec((D_MIX, D_MODEL), lambda i: (0, 0)),
            pl.BlockSpec((1, D_MODEL), lambda i: (0, 0)),
        ],
        out_specs=pl.BlockSpec((tm, D_MODEL), lambda i: (i, 0)),
        out_shape=jax.ShapeDtypeStruct((t, D_MODEL), F32),
        compiler_params=_params(("parallel",)),
        name="outproj",
    )(x2d, ya, yb, yc, w_bf16, g)


def _layernorm_rows(y, g, b):
    mu = jnp.mean(y, axis=-1, keepdims=True)
    yc = y - mu
    return yc * lax.rsqrt(jnp.mean(yc * yc, axis=-1, keepdims=True) + EPS) * g + b


def _conv_kernel(av_ref, avp_ref, avn_ref, ag_ref, agp_ref, agn_ref, gate_ref,
                 cw_ref, cb_ref, lg_ref, lb_ref, pw_ref, y_ref, ybuf, cbuf):
    i = pl.program_id(1)
    n = pl.num_programs(1)
    tm = av_ref.shape[0]

    def glu(a, g):
        return a.astype(F32) * jax.nn.sigmoid(g.astype(F32))

    ybuf[CONV_HALO:CONV_HALO + tm, :] = glu(av_ref[...], ag_ref[...])
    ybuf[0:CONV_HALO, :] = jnp.where(i > 0, glu(avp_ref[...], agp_ref[...]), 0.0)
    ybuf[CONV_HALO + tm:, :] = jnp.where(i < n - 1, glu(avn_ref[...], agn_ref[...]), 0.0)

    base = CONV_HALO - CONV_PAD
    for c in range(tm // CONV_ROWS):
        r0 = c * CONV_ROWS
        acc = jnp.broadcast_to(cb_ref[...], (CONV_ROWS, CONV_WIDTH))
        for j in range(CONV_KERNEL):
            acc = acc + cw_ref[j:j + 1, :] * ybuf[r0 + base + j:r0 + base + j + CONV_ROWS, :]
        cbuf[r0:r0 + CONV_ROWS, :] = acc

    z = jax.nn.silu(_layernorm_rows(cbuf[...], lg_ref[...], lb_ref[...]))
    z = jnp.dot(z.astype(BF16), pw_ref[...], preferred_element_type=F32)
    y_ref[...] = (z * jax.nn.silu(gate_ref[...].astype(F32))).astype(y_ref.dtype)


def _conv_branch(p, cw, cb, lg, lb, pw_bf16):
    b, s, _ = p.shape
    tm = TOKEN_TILE
    hb = tm // CONV_HALO
    nh = s // CONV_HALO

    def main(col):
        return pl.BlockSpec((None, tm, CONV_WIDTH), lambda bi, i: (bi, i, col))

    def prev(col):
        return pl.BlockSpec((None, CONV_HALO, CONV_WIDTH),
                            lambda bi, i: (bi, jnp.maximum(i * hb - 1, 0), col))

    def nxt(col):
        return pl.BlockSpec((None, CONV_HALO, CONV_WIDTH),
                            lambda bi, i: (bi, jnp.minimum((i + 1) * hb, nh - 1), col))

    def whole(shape):
        return pl.BlockSpec(shape, lambda bi, i: (0,) * len(shape))

    va, gl, gt = CONV_COLS
    return pl.pallas_call(
        _conv_kernel,
        grid=(b, s // tm),
        in_specs=[main(va), prev(va), nxt(va), main(gl), prev(gl), nxt(gl), main(gt),
                  whole((CONV_KERNEL, CONV_WIDTH)), whole((1, CONV_WIDTH)), whole((1, CONV_WIDTH)),
                  whole((1, CONV_WIDTH)), whole((CONV_WIDTH, CONV_WIDTH))],
        out_specs=pl.BlockSpec((None, tm, CONV_WIDTH), lambda bi, i: (bi, i, 0)),
        out_shape=jax.ShapeDtypeStruct((b, s, CONV_WIDTH), Y_DTYPE),
        scratch_shapes=[pltpu.VMEM((tm + 2 * CONV_HALO, CONV_WIDTH), F32),
                        pltpu.VMEM((tm, CONV_WIDTH), F32)],
        compiler_params=_params(("parallel", "parallel")),
        name="conv_branch",
    )(p, p, p, p, p, p, p, cw, cb, lg, lb, pw_bf16)


def _gelu_exact(x):
    return 0.5 * x * (1.0 + lax.erf(x * (2.0 ** -0.5)))


def _sgu_kernel(u_ref, v_ref, gate_ref, lg_ref, lb_ref, w_ref, bias_ref, y_ref):
    tm = u_ref.shape[0]
    col = lax.broadcasted_iota(jnp.int32, (SGU_CHUNK, SGU_WIDTH), 1)
    for c in range(tm // SGU_CHUNK):
        rows = slice(c * SGU_CHUNK, (c + 1) * SGU_CHUNK)
        v = _gelu_exact(v_ref[rows, :].astype(F32))
        v = _layernorm_rows(v, lg_ref[...], lb_ref[...])
        stacked = jnp.concatenate(
            [jnp.where((col >= h * SGU_HEAD_DIM) & (col < (h + 1) * SGU_HEAD_DIM), v, 0.0)
             for h in range(SGU_HEADS)], axis=0).astype(BF16)
        s = jnp.dot(w_ref[...], stacked, preferred_element_type=F32) + bias_ref[...]
        u = _gelu_exact(u_ref[rows, :].astype(F32))
        y_ref[rows, :] = (u * s * jax.nn.silu(gate_ref[rows, :].astype(F32))).astype(y_ref.dtype)


def _sgu_branch(p, lg, lb, w_cat_bf16, bias):
    b, s, _ = p.shape
    tm = TOKEN_TILE

    def main(col):
        return pl.BlockSpec((None, tm, SGU_WIDTH), lambda bi, i: (bi, i, col))

    def whole(shape):
        return pl.BlockSpec(shape, lambda bi, i: (0,) * len(shape))

    cu, cv, cg = SGU_COLS
    return pl.pallas_call(
        _sgu_kernel,
        grid=(b, s // tm),
        in_specs=[main(cu), main(cv), main(cg), whole((1, SGU_WIDTH)), whole((1, SGU_WIDTH)),
                  whole((SGU_CHUNK, SGU_HEADS * SGU_CHUNK)), whole((SGU_CHUNK, SGU_WIDTH))],
        out_specs=pl.BlockSpec((None, tm, SGU_WIDTH), lambda bi, i: (bi, i, 0)),
        out_shape=jax.ShapeDtypeStruct((b, s, SGU_WIDTH), Y_DTYPE),
        compiler_params=_params(("parallel", "parallel")),
        name="sgu_branch",
    )(p, p, p, lg, lb, w_cat_bf16, bias)


def _lower_bound_row(lb_ref, layer):
    p = lb_ref[...].astype(F32)
    m = jnp.max(p, axis=0, keepdims=True)
    e = jnp.exp(p - m)
    sm = e / jnp.sum(e, axis=0, keepdims=True)
    lb = jnp.zeros((1, p.shape[1]), F32)
    for j in range(1, layer + 1):
        lb = lb + sm[j:j + 1, :]
    return lb


def _decay_and_key(z, lb):
    e = jnp.exp(-jnp.abs(z))
    r = 1.0 / (1.0 + e)
    er = e * r
    pos = z >= 0
    sig = jnp.where(pos, r, er)
    nsig = jnp.where(pos, er, r)
    one_m = 1.0 - lb
    return lb + one_m * sig, one_m * nsig


def _inblock_scan(f, row, forward):
    a, b, t = f, jnp.ones_like(f), f
    per_level = []
    h = 1
    while h < HGRN_BLOCK:
        bit = (row & h) != 0
        per_level.append((a, b))
        partner = jnp.where(bit, pltpu.roll(t, h, 0), pltpu.roll(t, HGRN_BLOCK - h, 0))
        query_half = bit if forward else jnp.logical_not(bit)
        a = a * jnp.where(query_half, partner, 1.0)
        b = b * jnp.where(query_half, 1.0, partner)
        t = t * partner
        h *= 2
    return per_level, a, b, t


def _hgrn_kernel(q_ref, v_ref, zf_ref, zb_ref, gate_ref, lbf_ref, lbb_ref, ng_ref, y_ref,
                 qlv, klv, abuf, sball, sf, sb, *, layer):
    ph = pl.program_id(1)
    i = pl.program_id(2)
    n = pl.num_programs(2)
    cs = HGRN_CHUNK
    w = HGRN_WIDTH
    nb = cs // HGRN_BLOCK
    nl = HGRN_LEVELS
    n_in = 4
    slot_diag, slot_f, slot_b = nl, nl + 1, nl + 2
    a_f, b_f, t_f, a_b, b_b, t_b = range(6)

    lb_f = _lower_bound_row(lbf_ref, layer)
    lb_b = _lower_bound_row(lbb_ref, layer)
    row = lax.broadcasted_iota(jnp.int32, (HGRN_BLOCK, w), 0)

    def rows_of(blk):
        if isinstance(blk, int):
            return pl.ds(blk * HGRN_BLOCK, HGRN_BLOCK)
        return pl.ds(pl.multiple_of(blk * HGRN_BLOCK, HGRN_BLOCK), HGRN_BLOCK)

    def pair_totals(planes, hb):
        for blk in range(nb):
            if not (blk & hb):
                r0, r1 = rows_of(blk), rows_of(blk ^ hb)
                for plane in planes:
                    prod = abuf[plane, r0, :] * abuf[plane, r1, :]
                    abuf[plane, r0, :] = prod
                    abuf[plane, r1, :] = prod

    @pl.when(ph == 0)
    def _():
        c = n - 1 - i

        @pl.when(i == 0)
        def _():
            sb[...] = jnp.zeros_like(sb)

        sball[c] = sb[...].astype(BF16)

        def body(blk, carry):
            r = rows_of(blk)
            f, k = _decay_and_key(zb_ref[r, :].astype(F32), lb_b)
            _, _, b, t = _inblock_scan(f, row, forward=False)
            abuf[b_b, r, :] = b
            abuf[t_b, r, :] = t
            abuf[a_b, r, :] = k
            return carry

        lax.fori_loop(0, nb, body, 0)

        hb = 1
        while hb < nb:
            for blk in range(nb):
                if blk & hb:
                    r, rp = rows_of(blk), rows_of(blk ^ hb)
                    abuf[b_b, r, :] = abuf[b_b, r, :] * abuf[t_b, rp, :]
            pair_totals((t_b,), hb)
            hb *= 2

        kst = (abuf[a_b] * abuf[b_b]).astype(BF16)
        vb = v_ref[...].astype(BF16)
        tot = abuf[t_b, 0:1, :]
        for hd in range(HGRN_HEADS):
            cols = slice(hd * HGRN_DK, (hd + 1) * HGRN_DK)
            upd_s = lax.dot_general(vb[:, cols], kst[:, cols], (((0,), (0,)), ((), ())),
                                    preferred_element_type=F32)
            sb[hd] = sb[hd] * tot[:, cols] + upd_s

    @pl.when(ph == 1)
    def _():
        c = i

        @pl.when(i == 0)
        def _():
            sf[...] = jnp.zeros_like(sf)

        def body(blk, carry):
            r = rows_of(blk)
            q = q_ref[r, :].astype(F32)
            ff, kf = _decay_and_key(zf_ref[r, :].astype(F32), lb_f)
            fb, kb = _decay_and_key(zb_ref[r, :].astype(F32), lb_b)
            lv_f, af, bf_, tf = _inblock_scan(ff, row, forward=True)
            lv_b, ab, bb, tb = _inblock_scan(fb, row, forward=False)
            for lvl in range(n_in):
                bit = (row & (1 << lvl)) != 0
                qa = jnp.where(bit, lv_f[lvl][0], lv_b[lvl][0])
                kk = jnp.where(bit, kb * lv_b[lvl][1], kf * lv_f[lvl][1])
                qlv[lvl, r, :] = (q * qa).astype(BF16)
                klv[lvl, r, :] = kk.astype(BF16)
            qlv[slot_diag, r, :] = q.astype(BF16)
            klv[slot_diag, r, :] = (kf + kb).astype(BF16)
            for plane, val in ((a_f, af), (b_f, bf_), (t_f, tf), (a_b, ab), (b_b, bb), (t_b, tb)):
                abuf[plane, r, :] = val
            return carry

        lax.fori_loop(0, nb, body, 0)

        def level_operands(lvl, hb):
            for blk in range(nb):
                r = rows_of(blk)
                q = q_ref[r, :].astype(F32)
                if blk & hb:
                    _, kb = _decay_and_key(zb_ref[r, :].astype(F32), lb_b)
                    qlv[lvl, r, :] = (q * abuf[a_f, r, :]).astype(BF16)
                    klv[lvl, r, :] = (kb * abuf[b_b, r, :]).astype(BF16)
                else:
                    _, kf = _decay_and_key(zf_ref[r, :].astype(F32), lb_f)
                    qlv[lvl, r, :] = (q * abuf[a_b, r, :]).astype(BF16)
                    klv[lvl, r, :] = (kf * abuf[b_f, r, :]).astype(BF16)

        hb = 1
        lvl = n_in
        while hb < nb:
            level_operands(lvl, hb)
            for blk in range(nb):
                r, rp = rows_of(blk), rows_of(blk ^ hb)
                if blk & hb:
                    abuf[a_f, r, :] = abuf[a_f, r, :] * abuf[t_f, rp, :]
                    abuf[b_b, r, :] = abuf[b_b, r, :] * abuf[t_b, rp, :]
                else:
                    abuf[b_f, r, :] = abuf[b_f, r, :] * abuf[t_f, rp, :]
                    abuf[a_b, r, :] = abuf[a_b, r, :] * abuf[t_b, rp, :]
            pair_totals((t_f, t_b), hb)
            hb *= 2
            lvl += 1

        q_all = q_ref[...].astype(F32)
        _, kf_all = _decay_and_key(zf_ref[...].astype(F32), lb_f)
        qlv[slot_f] = (q_all * abuf[a_f]).astype(BF16)
        qlv[slot_b] = (q_all * abuf[a_b]).astype(BF16)
        klv[slot_f] = (kf_all * abuf[b_f]).astype(BF16)
        tot_f = abuf[t_f, 0:1, :]

        ti = lax.broadcasted_iota(jnp.int32, (cs, cs), 0)
        si = lax.broadcasted_iota(jnp.int32, (cs, cs), 1)
        x = ti ^ si
        level_id = jnp.where(x == 0, slot_diag, 31 - lax.clz(x))

        vb = v_ref[...].astype(BF16)
        ng = ng_ref[...]
        nt_dims = (((1,), (1,)), ((), ()))
        for hd in range(HGRN_HEADS):
            cols = slice(hd * HGRN_DK, (hd + 1) * HGRN_DK)
            scores = jnp.zeros((cs, cs), F32)
            for lvl in range(nl + 1):
                s_l = lax.dot_general(qlv[lvl, :, cols], klv[lvl, :, cols], nt_dims,
                                      preferred_element_type=F32)
                scores = jnp.where(level_id == lvl, s_l, scores)
            o = jnp.dot(scores.astype(BF16), vb[:, cols], preferred_element_type=F32)
            q_in = jnp.concatenate([qlv[slot_f, :, cols], qlv[slot_b, :, cols]], axis=1)
            s_in = jnp.concatenate([sf[hd].astype(BF16), sball[c, hd]], axis=1)
            o = o + lax.dot_general(q_in, s_in, nt_dims, preferred_element_type=F32)
            o = o * lax.rsqrt(jnp.mean(o * o, axis=-1, keepdims=True) + EPS) * ng
            y_ref[:, cols] = (o * jax.nn.silu(gate_ref[:, cols].astype(F32))).astype(y_ref.dtype)
            upd_s = lax.dot_general(vb[:, cols], klv[slot_f, :, cols], (((0,), (0,)), ((), ())),
                                    preferred_element_type=F32)
            sf[hd] = sf[hd] * tot_f[:, cols] + upd_s


def _hgrn_branch(p, lb_fwd, lb_bwd, norm_g, layer):
    b, s, _ = p.shape
    cs = HGRN_CHUNK
    n = s // cs
    cq, ci, cff, cfb, cg = HGRN_COLS

    def both(col):
        return pl.BlockSpec((None, cs, HGRN_WIDTH),
                            lambda bi, ph, i: (bi, jnp.where(ph == 0, n - 1 - i, i), col))

    def second(col):
        return pl.BlockSpec((None, cs, HGRN_WIDTH), lambda bi, ph, i: (bi, ph * i, col))

    def whole(shape):
        return pl.BlockSpec(shape, lambda bi, ph, i: (0,) * len(shape))

    n_slots = HGRN_LEVELS + 3
    return pl.pallas_call(
        functools.partial(_hgrn_kernel, layer=layer),
        grid=(b, 2, n),
        in_specs=[second(cq), both(ci), second(cff), both(cfb), second(cg),
                  whole((DEPTH, HGRN_WIDTH)), whole((DEPTH, HGRN_WIDTH)), whole((1, HGRN_DK))],
        out_specs=pl.BlockSpec((None, cs, HGRN_WIDTH), lambda bi, ph, i: (bi, ph * i, 0)),
        out_shape=jax.ShapeDtypeStruct((b, s, HGRN_WIDTH), Y_DTYPE),
        scratch_shapes=[
            pltpu.VMEM((n_slots, cs, HGRN_WIDTH), BF16),
            pltpu.VMEM((n_slots, cs, HGRN_WIDTH), BF16),
            pltpu.VMEM((6, cs, HGRN_WIDTH), F32),
            pltpu.VMEM((n, HGRN_HEADS, HGRN_DK, HGRN_DK), BF16),
            pltpu.VMEM((HGRN_HEADS, HGRN_DK, HGRN_DK), F32),
            pltpu.VMEM((HGRN_HEADS, HGRN_DK, HGRN_DK), F32),
        ],
        compiler_params=_params(("arbitrary", "arbitrary", "arbitrary")),
        name="hgrn_branch",
    )(p, p, p, p, p, lb_fwd, lb_bwd, norm_g)


def _trunk(x, prm):
    b, s, d = x.shape
    x2d = x.reshape(b * s, d)
    for l in range(DEPTH):
        p = _inproj(x2d, prm["norm_g"][l], prm["w_in"][l]).reshape(b, s, D_IN)
        ya = _conv_branch(p, prm["conv_w"][l], prm["conv_b"][l], prm["conv_ln_g"][l],
                          prm["conv_ln_b"][l], prm["conv_pw"][l])
        yb = _sgu_branch(p, prm["sgu_ln_g"][l], prm["sgu_ln_b"][l], prm["sgu_w"][l], prm["sgu_b"][l])
        yc = _hgrn_branch(p, prm["hgrn_lb_fwd"], prm["hgrn_lb_bwd"], prm["hgrn_norm_g"][l], l)
        x2d = _outproj(x2d, ya.reshape(b * s, -1), yb.reshape(b * s, -1), yc.reshape(b * s, -1),
                       prm["w_out"][l], prm["final_norm_g"], final=(l == DEPTH - 1))
    return x2d.reshape(b, s, d)


def kernel(x_prompt, x_sample, norm_g, w_in, conv_w, conv_b, conv_ln_g, conv_ln_b, conv_pw,
           sgu_ln_g, sgu_ln_b, sgu_w, sgu_b, hgrn_lb_fwd, hgrn_lb_bwd, hgrn_norm_g,
           w_out, final_norm_g):
    row = lambda a: a[:, None, :]
    prm = {
        "norm_g": row(norm_g),
        "w_in": w_in.astype(BF16),
        "conv_w": conv_w,
        "conv_b": row(conv_b),
        "conv_ln_g": row(conv_ln_g),
        "conv_ln_b": row(conv_ln_b),
        "conv_pw": conv_pw.astype(BF16),
        "sgu_ln_g": row(sgu_ln_g),
        "sgu_ln_b": row(sgu_ln_b),
        "sgu_w": sgu_w.transpose(0, 2, 1, 3).reshape(DEPTH, SGU_CHUNK, SGU_HEADS * SGU_CHUNK).astype(BF16),
        "sgu_b": jnp.repeat(sgu_b.transpose(0, 2, 1), SGU_HEAD_DIM, axis=2),
        "hgrn_lb_fwd": hgrn_lb_fwd,
        "hgrn_lb_bwd": hgrn_lb_bwd,
        "hgrn_norm_g": row(hgrn_norm_g),
        "w_out": w_out.astype(BF16),
        "final_norm_g": final_norm_g[None, :],
    }
    return (_trunk(x_prompt, prm), _trunk(x_sample, prm))
```

```python
import functools

import jax
import jax.numpy as jnp
from jax import lax
from jax.experimental import pallas as pl
from jax.experimental.pallas import tpu as pltpu

F32 = jnp.float32
BF16 = jnp.bfloat16

D_MODEL = 1024
DEPTH = 2
CONV_WIDTH = 256
CONV_KERNEL = 31
CONV_PAD = CONV_KERNEL // 2
SGU_WIDTH = 256
SGU_HEADS = 4
SGU_HEAD_DIM = SGU_WIDTH // SGU_HEADS
SGU_CHUNK = 128
HGRN_WIDTH = 512
HGRN_HEADS = 4
HGRN_DK = 128
D_IN = 4096
D_MIX = 1024
EPS = 1e-6

CONV_COLS = (0, 1, 2)
SGU_COLS = (3, 4, 5)
HGRN_COLS = (3, 4, 5, 6, 7)

P_DTYPE = BF16
Y_DTYPE = BF16

V7X_VMEM_LIMIT_BYTES = 56 * 1024 * 1024
SUBLANES = 8

TOKEN_TILE = 512
CONV_HALO = 16
CONV_ROWS = 32
HGRN_CHUNK = 128
HGRN_CHUNKS_PER_STEP = 4
HGRN_BLOCK = 16
HGRN_LEVELS = 7


def _params(sem):
    return pltpu.CompilerParams(dimension_semantics=sem, vmem_limit_bytes=V7X_VMEM_LIMIT_BYTES)


def _inproj_kernel(x_ref, g_ref, w_ref, p_ref):
    x = x_ref[...]
    ms = jnp.mean(x * x, axis=-1, keepdims=True)
    h = (x * lax.rsqrt(ms + EPS) * g_ref[...]).astype(BF16)
    tn = 1024
    for j in range(D_IN // tn):
        p_ref[:, j * tn:(j + 1) * tn] = jnp.dot(
            h, w_ref[:, j * tn:(j + 1) * tn], preferred_element_type=F32).astype(p_ref.dtype)


def _inproj(x2d, g, w_bf16):
    t = x2d.shape[0]
    tm = TOKEN_TILE
    return pl.pallas_call(
        _inproj_kernel,
        grid=(t // tm,),
        in_specs=[
            pl.BlockSpec((tm, D_MODEL), lambda i: (i, 0)),
            pl.BlockSpec((1, D_MODEL), lambda i: (0, 0)),
            pl.BlockSpec((D_MODEL, D_IN), lambda i: (0, 0)),
        ],
        out_specs=pl.BlockSpec((tm, D_IN), lambda i: (i, 0)),
        out_shape=jax.ShapeDtypeStruct((t, D_IN), P_DTYPE),
        compiler_params=_params(("parallel",)),
        name="inproj",
    )(x2d, g, w_bf16)


def _outproj_kernel(x_ref, ya_ref, yb_ref, yc_ref, w_ref, g_ref, o_ref, *, final):
    acc = x_ref[...]
    acc = acc + jnp.dot(ya_ref[...], w_ref[0:CONV_WIDTH, :], preferred_element_type=F32)
    acc = acc + jnp.dot(yb_ref[...], w_ref[CONV_WIDTH:CONV_WIDTH + SGU_WIDTH, :],
                        preferred_element_type=F32)
    acc = acc + jnp.dot(yc_ref[...], w_ref[CONV_WIDTH + SGU_WIDTH:, :], preferred_element_type=F32)
    if final:
        ms = jnp.mean(acc * acc, axis=-1, keepdims=True)
        acc = acc * lax.rsqrt(ms + EPS) * g_ref[...]
    o_ref[...] = acc


def _outproj(x2d, ya, yb, yc, w_bf16, g, final):
    t = x2d.shape[0]
    tm = TOKEN_TILE
    return pl.pallas_call(
        functools.partial(_outproj_kernel, final=final),
        grid=(t // tm,),
        in_specs=[
            pl.BlockSpec((tm, D_MODEL), lambda i: (i, 0)),
            pl.BlockSpec((tm, CONV_WIDTH), lambda i: (i, 0)),
            pl.BlockSpec((tm, SGU_WIDTH), lambda i: (i, 0)),
            pl.BlockSpec((tm, HGRN_WIDTH), lambda i: (i, 0)),
            pl.BlockSpec((D_MIX, D_MODEL), lambda i: (0, 0)),
            pl.BlockSpec((1, D_MODEL), lambda i: (0, 0)),
        ],
        out_specs=pl.BlockSpec((tm, D_MODEL), lambda i: (i, 0)),
        out_shape=jax.ShapeDtypeStruct((t, D_MODEL), F32),
        compiler_params=_params(("parallel",)),
        name="outproj",
    )(x2d, ya, yb, yc, w_bf16, g)


def _layernorm_rows(y, g, b):
    mu = jnp.mean(y, axis=-1, keepdims=True)
    yc = y - mu
    return yc * lax.rsqrt(jnp.mean(yc * yc, axis=-1, keepdims=True) + EPS) * g + b


def _conv_kernel(av_ref, avp_ref, avn_ref, ag_ref, agp_ref, agn_ref, gate_ref,
                 cw_ref, cb_ref, lg_ref, lb_ref, pw_ref, y_ref, ybuf, cbuf):
    i = pl.program_id(1)
    n = pl.num_programs(1)
    tm = av_ref.shape[0]

    def glu(a, g):
        return a.astype(F32) * jax.nn.sigmoid(g.astype(F32))

    ybuf[0, CONV_HALO:CONV_HALO + tm, :] = glu(av_ref[...], ag_ref[...])
    ybuf[0, 0:CONV_HALO, :] = jnp.where(i > 0, glu(avp_ref[...], agp_ref[...]), 0.0)
    ybuf[0, CONV_HALO + tm:, :] = jnp.where(i < n - 1, glu(avn_ref[...], agn_ref[...]), 0.0)
    n_sh = tm + 2 * CONV_HALO - SUBLANES
    for r in range(1, SUBLANES):
        ybuf[r, 0:n_sh, :] = ybuf[0, r:r + n_sh, :]

    base = CONV_HALO - CONV_PAD
    for c in range(tm // CONV_ROWS):
        r0 = c * CONV_ROWS
        acc = jnp.broadcast_to(cb_ref[...], (CONV_ROWS, CONV_WIDTH))
        for j in range(CONV_KERNEL):
            m, r = divmod(base + j, SUBLANES)
            start = r0 + m * SUBLANES
            acc = acc + cw_ref[j:j + 1, :] * ybuf[r, start:start + CONV_ROWS, :]
        cbuf[r0:r0 + CONV_ROWS, :] = acc

    z = jax.nn.silu(_layernorm_rows(cbuf[...], lg_ref[...], lb_ref[...]))
    z = jnp.dot(z.astype(BF16), pw_ref[...], preferred_element_type=F32)
    y_ref[...] = (z * jax.nn.silu(gate_ref[...].astype(F32))).astype(y_ref.dtype)


def _conv_branch(p, cw, cb, lg, lb, pw_bf16):
    b, s, _ = p.shape
    tm = TOKEN_TILE
    hb = tm // CONV_HALO
    nh = s // CONV_HALO

    def main(col):
        return pl.BlockSpec((None, tm, CONV_WIDTH), lambda bi, i: (bi, i, col))

    def prev(col):
        return pl.BlockSpec((None, CONV_HALO, CONV_WIDTH),
                            lambda bi, i: (bi, jnp.maximum(i * hb - 1, 0), col))

    def nxt(col):
        return pl.BlockSpec((None, CONV_HALO, CONV_WIDTH),
                            lambda bi, i: (bi, jnp.minimum((i + 1) * hb, nh - 1), col))

    def whole(shape):
        return pl.BlockSpec(shape, lambda bi, i: (0,) * len(shape))

    va, gl, gt = CONV_COLS
    return pl.pallas_call(
        _conv_kernel,
        grid=(b, s // tm),
        in_specs=[main(va), prev(va), nxt(va), main(gl), prev(gl), nxt(gl), main(gt),
                  whole((CONV_KERNEL, CONV_WIDTH)), whole((1, CONV_WIDTH)), whole((1, CONV_WIDTH)),
                  whole((1, CONV_WIDTH)), whole((CONV_WIDTH, CONV_WIDTH))],
        out_specs=pl.BlockSpec((None, tm, CONV_WIDTH), lambda bi, i: (bi, i, 0)),
        out_shape=jax.ShapeDtypeStruct((b, s, CONV_WIDTH), Y_DTYPE),
        scratch_shapes=[pltpu.VMEM((SUBLANES, tm + 2 * CONV_HALO, CONV_WIDTH), F32),
                        pltpu.VMEM((tm, CONV_WIDTH), F32)],
        compiler_params=_params(("parallel", "parallel")),
        name="conv_branch",
    )(p, p, p, p, p, p, p, cw, cb, lg, lb, pw_bf16)


def _gelu_exact(x):
    return 0.5 * x * (1.0 + lax.erf(x * (2.0 ** -0.5)))


def _sgu_kernel(u_ref, v_ref, gate_ref, lg_ref, lb_ref, w_ref, bias_ref, y_ref):
    tm = u_ref.shape[0]
    col = lax.broadcasted_iota(jnp.int32, (SGU_CHUNK, SGU_WIDTH), 1)
    for c in range(tm // SGU_CHUNK):
        rows = slice(c * SGU_CHUNK, (c + 1) * SGU_CHUNK)
        v = _gelu_exact(v_ref[rows, :].astype(F32))
        v = _layernorm_rows(v, lg_ref[...], lb_ref[...])
        stacked = jnp.concatenate(
            [jnp.where((col >= h * SGU_HEAD_DIM) & (col < (h + 1) * SGU_HEAD_DIM), v, 0.0)
             for h in range(SGU_HEADS)], axis=0).astype(BF16)
        s = jnp.dot(w_ref[...], stacked, preferred_element_type=F32) + bias_ref[...]
        u = _gelu_exact(u_ref[rows, :].astype(F32))
        y_ref[rows, :] = (u * s * jax.nn.silu(gate_ref[rows, :].astype(F32))).astype(y_ref.dtype)


def _sgu_branch(p, lg, lb, w_cat_bf16, bias):
    b, s, _ = p.shape
    tm = TOKEN_TILE

    def main(col):
        return pl.BlockSpec((None, tm, SGU_WIDTH), lambda bi, i: (bi, i, col))

    def whole(shape):
        return pl.BlockSpec(shape, lambda bi, i: (0,) * len(shape))

    cu, cv, cg = SGU_COLS
    return pl.pallas_call(
        _sgu_kernel,
        grid=(b, s // tm),
        in_specs=[main(cu), main(cv), main(cg), whole((1, SGU_WIDTH)), whole((1, SGU_WIDTH)),
                  whole((SGU_CHUNK, SGU_HEADS * SGU_CHUNK)), whole((SGU_CHUNK, SGU_WIDTH))],
        out_specs=pl.BlockSpec((None, tm, SGU_WIDTH), lambda bi, i: (bi, i, 0)),
        out_shape=jax.ShapeDtypeStruct((b, s, SGU_WIDTH), Y_DTYPE),
        compiler_params=_params(("parallel", "parallel")),
        name="sgu_branch",
    )(p, p, p, lg, lb, w_cat_bf16, bias)


def _lower_bound_row(lb_ref, layer):
    p = lb_ref[...].astype(F32)
    m = jnp.max(p, axis=0, keepdims=True)
    e = jnp.exp(p - m)
    sm = e / jnp.sum(e, axis=0, keepdims=True)
    lb = jnp.zeros((1, p.shape[1]), F32)
    for j in range(1, layer + 1):
        lb = lb + sm[j:j + 1, :]
    return lb


def _decay_and_key(z, lb):
    f = lb + (1.0 - lb) * (1.0 / (1.0 + jnp.exp(-z)))
    return f, 1.0 - f


def _inblock_scan(f, row, forward):
    half = HGRN_BLOCK // 2
    a, b, t = list(f), [jnp.ones_like(f[0]), jnp.ones_like(f[1])], list(f)
    per_level = []
    h = 1
    while h < half:
        bit = (row & h) != 0
        query_half = bit if forward else jnp.logical_not(bit)
        per_level.append((tuple(a), tuple(b)))
        for s in range(2):
            partner = jnp.where(bit, pltpu.roll(t[s], h, 0), pltpu.roll(t[s], half - h, 0))
            a[s] = a[s] * jnp.where(query_half, partner, 1.0)
            b[s] = b[s] * jnp.where(query_half, 1.0, partner)
            t[s] = t[s] * partner
        h *= 2
    per_level.append((tuple(a), tuple(b)))
    if forward:
        a[1] = a[1] * t[0]
        b[0] = b[0] * t[1]
    else:
        a[0] = a[0] * t[1]
        b[1] = b[1] * t[0]
    total = t[0] * t[1]
    return per_level, a, b, total


def _hgrn_kernel(q_ref, v_ref, zf_ref, zb_ref, gate_ref, lbf_ref, lbb_ref, ng_ref, y_ref,
                 qlv, klv, abuf, sball, sf, sb, *, layer):
    ph = pl.program_id(1)
    i = pl.program_id(2)
    n = pl.num_programs(2)
    cs = HGRN_CHUNK
    w = HGRN_WIDTH
    half = HGRN_BLOCK // 2
    g = q_ref.shape[0] // cs
    nb = cs // HGRN_BLOCK
    nl = HGRN_LEVELS
    n_in = 4
    slot_diag, slot_f, slot_b = nl, nl + 1, nl + 2
    a_f, b_f, t_f, a_b, b_b, t_b, q32, k_f, k_b = range(9)

    lb_f = _lower_bound_row(lbf_ref, layer)
    lb_b = _lower_bound_row(lbb_ref, layer)
    row = lax.broadcasted_iota(jnp.int32, (half, w), 0)

    def halves(x):
        return [x[0:half, :], x[half:, :]]

    def whole(lo_hi):
        return jnp.concatenate(list(lo_hi), axis=0)

    def pair_totals(planes, hb):
        for blk in range(nb):
            if not (blk & hb):
                r0 = pl.ds(blk * HGRN_BLOCK, HGRN_BLOCK)
                r1 = pl.ds((blk ^ hb) * HGRN_BLOCK, HGRN_BLOCK)
                for plane in planes:
                    prod = abuf[plane, r0, :] * abuf[plane, r1, :]
                    abuf[plane, r0, :] = prod
                    abuf[plane, r1, :] = prod

    def sweep0_chunk(base, c):
        sball[c] = sb[...].astype(BF16)

        def body(blk, carry):
            rl = pl.ds(pl.multiple_of(blk * HGRN_BLOCK, HGRN_BLOCK), HGRN_BLOCK)
            rg = pl.ds(pl.multiple_of(base + blk * HGRN_BLOCK, HGRN_BLOCK), HGRN_BLOCK)
            f, k = _decay_and_key(zb_ref[rg, :].astype(F32), lb_b)
            _, _, b, t = _inblock_scan(halves(f), row, forward=False)
            abuf[b_b, rl, :] = whole(b)
            abuf[t_b, rl, :] = whole((t, t))
            abuf[k_b, rl, :] = k
            return carry

        lax.fori_loop(0, nb, body, 0)

        hb = 1
        while hb < nb:
            for blk in range(nb):
                if blk & hb:
                    r = pl.ds(blk * HGRN_BLOCK, HGRN_BLOCK)
                    rp = pl.ds((blk ^ hb) * HGRN_BLOCK, HGRN_BLOCK)
                    abuf[b_b, r, :] = abuf[b_b, r, :] * abuf[t_b, rp, :]
            pair_totals((t_b,), hb)
            hb *= 2

        kst = (abuf[k_b] * abuf[b_b]).astype(BF16)
        vb = v_ref[pl.ds(base, cs), :].astype(BF16)
        tot = abuf[t_b, 0:1, :]
        for hd in range(HGRN_HEADS):
            cols = slice(hd * HGRN_DK, (hd + 1) * HGRN_DK)
            upd_s = lax.dot_general(vb[:, cols], kst[:, cols], (((0,), (0,)), ((), ())),
                                    preferred_element_type=F32)
            sb[hd] = sb[hd] * tot[:, cols] + upd_s

    def sweep1_chunk(base, c):
        def body(blk, carry):
            rl = pl.ds(pl.multiple_of(blk * HGRN_BLOCK, HGRN_BLOCK), HGRN_BLOCK)
            rg = pl.ds(pl.multiple_of(base + blk * HGRN_BLOCK, HGRN_BLOCK), HGRN_BLOCK)
            q16 = q_ref[rg, :].astype(F32)
            ff, kf = _decay_and_key(zf_ref[rg, :].astype(F32), lb_f)
            fb, kb = _decay_and_key(zb_ref[rg, :].astype(F32), lb_b)
            q, kf2, kb2 = halves(q16), halves(kf), halves(kb)
            lv_f, af, bf_, tf = _inblock_scan(halves(ff), row, forward=True)
            lv_b, ab, bb, tb = _inblock_scan(halves(fb), row, forward=False)
            for lvl in range(n_in - 1):
                bit = (row & (1 << lvl)) != 0
                (fa, fbk), (ba, bbk) = lv_f[lvl], lv_b[lvl]
                qo = [q[s] * jnp.where(bit, fa[s], ba[s]) for s in range(2)]
                ko = [jnp.where(bit, kb2[s] * bbk[s], kf2[s] * fbk[s]) for s in range(2)]
                qlv[lvl, rl, :] = whole(qo).astype(BF16)
                klv[lvl, rl, :] = whole(ko).astype(BF16)
            (fa, fbk), (ba, bbk) = lv_f[n_in - 1], lv_b[n_in - 1]
            qlv[n_in - 1, rl, :] = whole((q[0] * ba[0], q[1] * fa[1])).astype(BF16)
            klv[n_in - 1, rl, :] = whole((kf2[0] * fbk[0], kb2[1] * bbk[1])).astype(BF16)
            qlv[slot_diag, rl, :] = q16.astype(BF16)
            klv[slot_diag, rl, :] = (kf + kb).astype(BF16)
            for plane, val in ((a_f, whole(af)), (b_f, whole(bf_)), (t_f, whole((tf, tf))),
                               (a_b, whole(ab)), (b_b, whole(bb)), (t_b, whole((tb, tb))),
                               (q32, q16), (k_f, kf), (k_b, kb)):
                abuf[plane, rl, :] = val
            return carry

        lax.fori_loop(0, nb, body, 0)

        hb = 1
        lvl = n_in
        while hb < nb:
            for blk in range(nb):
                r = pl.ds(blk * HGRN_BLOCK, HGRN_BLOCK)
                rp = pl.ds((blk ^ hb) * HGRN_BLOCK, HGRN_BLOCK)
                if blk & hb:
                    qa, kb_ = abuf[a_f, r, :], abuf[b_b, r, :]
                    qlv[lvl, r, :] = (abuf[q32, r, :] * qa).astype(BF16)
                    klv[lvl, r, :] = (abuf[k_b, r, :] * kb_).astype(BF16)
                    abuf[a_f, r, :] = qa * abuf[t_f, rp, :]
                    abuf[b_b, r, :] = kb_ * abuf[t_b, rp, :]
                else:
                    qa, kb_ = abuf[a_b, r, :], abuf[b_f, r, :]
                    qlv[lvl, r, :] = (abuf[q32, r, :] * qa).astype(BF16)
                    klv[lvl, r, :] = (abuf[k_f, r, :] * kb_).astype(BF16)
                    abuf[a_b, r, :] = qa * abuf[t_b, rp, :]
                    abuf[b_f, r, :] = kb_ * abuf[t_f, rp, :]
            pair_totals((t_f, t_b), hb)
            hb *= 2
            lvl += 1

        q_all = abuf[q32]
        qlv[slot_f] = (q_all * abuf[a_f]).astype(BF16)
        qlv[slot_b] = (q_all * abuf[a_b]).astype(BF16)
        klv[slot_f] = (abuf[k_f] * abuf[b_f]).astype(BF16)
        tot_f = abuf[t_f, 0:1, :]

        ti = lax.broadcasted_iota(jnp.int32, (cs, cs), 0)
        si = lax.broadcasted_iota(jnp.int32, (cs, cs), 1)
        x = ti ^ si
        level_id = jnp.where(x == 0, slot_diag, 31 - lax.clz(x))

        rows = pl.ds(base, cs)
        vb = v_ref[rows, :].astype(BF16)
        ng = ng_ref[...]
        nt_dims = (((1,), (1,)), ((), ()))
        heads = [slice(hd * HGRN_DK, (hd + 1) * HGRN_DK) for hd in range(HGRN_HEADS)]
        scores = [None] * HGRN_HEADS
        for lvl in range(nl + 1):
            mask = level_id == lvl
            for hd, cols in enumerate(heads):
                s_l = lax.dot_general(qlv[lvl, :, cols], klv[lvl, :, cols], nt_dims,
                                      preferred_element_type=F32)
                scores[hd] = s_l if lvl == 0 else jnp.where(mask, s_l, scores[hd])
        for hd, cols in enumerate(heads):
            o = jnp.dot(scores[hd].astype(BF16), vb[:, cols], preferred_element_type=F32)
            q_in = jnp.concatenate([qlv[slot_f, :, cols], qlv[slot_b, :, cols]], axis=1)
            s_in = jnp.concatenate([sf[hd].astype(BF16), sball[c, hd]], axis=1)
            o = o + lax.dot_general(q_in, s_in, nt_dims, preferred_element_type=F32)
            o = o * lax.rsqrt(jnp.mean(o * o, axis=-1, keepdims=True) + EPS) * ng
            y_ref[rows, cols] = (o * jax.nn.silu(gate_ref[rows, cols].astype(F32))).astype(y_ref.dtype)
            upd_s = lax.dot_general(vb[:, cols], klv[slot_f, :, cols], (((0,), (0,)), ((), ())),
                                    preferred_element_type=F32)
            sf[hd] = sf[hd] * tot_f[:, cols] + upd_s

    @pl.when(ph == 0)
    def _():
        @pl.when(i == 0)
        def _():
            sb[...] = jnp.zeros_like(sb)

        def chunk(j, carry):
            jj = g - 1 - j
            sweep0_chunk(pl.multiple_of(jj * cs, cs), (n - 1 - i) * g + jj)
            return carry

        lax.fori_loop(0, g, chunk, 0)

    @pl.when(ph == 1)
    def _():
        @pl.when(i == 0)
        def _():
            sf[...] = jnp.zeros_like(sf)

        def chunk(j, carry):
            sweep1_chunk(pl.multiple_of(j * cs, cs), i * g + j)
            return carry

        lax.fori_loop(0, g, chunk, 0)


def _hgrn_branch(p, lb_fwd, lb_bwd, norm_g, layer):
    b, s, _ = p.shape
    cs = HGRN_CHUNK
    rows = HGRN_CHUNKS_PER_STEP * cs
    n = s // rows
    cq, ci, cff, cfb, cg = HGRN_COLS

    def both(col):
        return pl.BlockSpec((None, rows, HGRN_WIDTH),
                            lambda bi, ph, i: (bi, jnp.where(ph == 0, n - 1 - i, i), col))

    def second(col):
        return pl.BlockSpec((None, rows, HGRN_WIDTH), lambda bi, ph, i: (bi, ph * i, col))

    def whole(shape):
        return pl.BlockSpec(shape, lambda bi, ph, i: (0,) * len(shape))

    n_slots = HGRN_LEVELS + 3
    return pl.pallas_call(
        functools.partial(_hgrn_kernel, layer=layer),
        grid=(b, 2, n),
        in_specs=[second(cq), both(ci), second(cff), both(cfb), second(cg),
                  whole((DEPTH, HGRN_WIDTH)), whole((DEPTH, HGRN_WIDTH)), whole((1, HGRN_DK))],
        out_specs=pl.BlockSpec((None, rows, HGRN_WIDTH), lambda bi, ph, i: (bi, ph * i, 0)),
        out_shape=jax.ShapeDtypeStruct((b, s, HGRN_WIDTH), Y_DTYPE),
        scratch_shapes=[
            pltpu.VMEM((n_slots, cs, HGRN_WIDTH), BF16),
            pltpu.VMEM((n_slots, cs, HGRN_WIDTH), BF16),
            pltpu.VMEM((9, cs, HGRN_WIDTH), F32),
            pltpu.VMEM((s // cs, HGRN_HEADS, HGRN_DK, HGRN_DK), BF16),
            pltpu.VMEM((HGRN_HEADS, HGRN_DK, HGRN_DK), F32),
            pltpu.VMEM((HGRN_HEADS, HGRN_DK, HGRN_DK), F32),
        ],
        compiler_params=_params(("arbitrary", "arbitrary", "arbitrary")),
        name="hgrn_branch",
    )(p, p, p, p, p, lb_fwd, lb_bwd, norm_g)


def _trunk(x, prm):
    b, s, d = x.shape
    x2d = x.reshape(b * s, d)
    for l in range(DEPTH):
        p = _inproj(x2d, prm["norm_g"][l], prm["w_in"][l]).reshape(b, s, D_IN)
        ya = _conv_branch(p, prm["conv_w"][l], prm["conv_b"][l], prm["conv_ln_g"][l],
                          prm["conv_ln_b"][l], prm["conv_pw"][l])
        yb = _sgu_branch(p, prm["sgu_ln_g"][l], prm["sgu_ln_b"][l], prm["sgu_w"][l], prm["sgu_b"][l])
        yc = _hgrn_branch(p, prm["hgrn_lb_fwd"], prm["hgrn_lb_bwd"], prm["hgrn_norm_g"][l], l)
        x2d = _outproj(x2d, ya.reshape(b * s, -1), yb.reshape(b * s, -1), yc.reshape(b * s, -1),
                       prm["w_out"][l], prm["final_norm_g"], final=(l == DEPTH - 1))
    return x2d.reshape(b, s, d)


def kernel(x_prompt, x_sample, norm_g, w_in, conv_w, conv_b, conv_ln_g, conv_ln_b, conv_pw,
           sgu_ln_g, sgu_ln_b, sgu_w, sgu_b, hgrn_lb_fwd, hgrn_lb_bwd, hgrn_norm_g,
           w_out, final_norm_g):
    row = lambda a: a[:, None, :]
    prm = {
        "norm_g": row(norm_g),
        "w_in": w_in.astype(BF16),
        "conv_w": conv_w,
        "conv_b": row(conv_b),
        "conv_ln_g": row(conv_ln_g),
        "conv_ln_b": row(conv_ln_b),
        "conv_pw": conv_pw.astype(BF16),
        "sgu_ln_g": row(sgu_ln_g),
        "sgu_ln_b": row(sgu_ln_b),
        "sgu_w": sgu_w.transpose(0, 2, 1, 3).reshape(DEPTH, SGU_CHUNK, SGU_HEADS * SGU_CHUNK).astype(BF16),
        "sgu_b": jnp.repeat(sgu_b.transpose(0, 2, 1), SGU_HEAD_DIM, axis=2),
        "hgrn_lb_fwd": hgrn_lb_fwd,
        "hgrn_lb_bwd": hgrn_lb_bwd,
        "hgrn_norm_g": row(hgrn_norm_g),
        "w_out": w_out.astype(BF16),
        "final_norm_g": final_norm_g[None, :],
    }
    return (_trunk(x_prompt, prm), _trunk(x_sample, prm))
```

```python
import functools

import jax
import jax.numpy as jnp
from jax import lax
from jax.experimental import pallas as pl
from jax.experimental.pallas import tpu as pltpu

F32 = jnp.float32
BF16 = jnp.bfloat16

D_MODEL = 1024
DEPTH = 2
CONV_WIDTH = 256
CONV_KERNEL = 31
CONV_PAD = CONV_KERNEL // 2
SGU_WIDTH = 256
SGU_HEADS = 4
SGU_HEAD_DIM = SGU_WIDTH // SGU_HEADS
SGU_CHUNK = 128
HGRN_WIDTH = 512
HGRN_HEADS = 4
HGRN_DK = 128
D_IN = 4096
D_MIX = 1024
EPS = 1e-6

CONV_COLS = (0, 1, 2)
SGU_COLS = (3, 4, 5)
HGRN_COLS = (0, 1, 2, 3, 4)
FRONT_COLS = 3 * CONV_WIDTH + 3 * SGU_WIDTH
HGRN_IN = D_IN - FRONT_COLS

P_DTYPE = BF16
Y_DTYPE = BF16

V7X_VMEM_LIMIT_BYTES = 56 * 1024 * 1024
SUBLANES = 8

TOKEN_TILE = 512
CONV_HALO = 16
CONV_ROWS = 32
HGRN_CHUNK = 128
HGRN_CHUNKS_PER_STEP = 4
HGRN_BLOCK = 16
HGRN_LEVELS = 7


def _params(sem):
    return pltpu.CompilerParams(dimension_semantics=sem, vmem_limit_bytes=V7X_VMEM_LIMIT_BYTES)


def _layernorm_rows(y, g, b):
    mu = jnp.mean(y, axis=-1, keepdims=True)
    yc = y - mu
    return yc * lax.rsqrt(jnp.mean(yc * yc, axis=-1, keepdims=True) + EPS) * g + b


def _gelu_exact(x):
    return 0.5 * x * (1.0 + lax.erf(x * (2.0 ** -0.5)))


def _front_kernel(x_ref, g_ref, w_ref, cw_ref, cb_ref, clg_ref, clb_ref, pw_ref,
                  slg_ref, slb_ref, sw_ref, sbias_ref, ph_ref, ya_ref, yb_ref,
                  pcur, pprev, tail, ybuf, cbuf):
    i = pl.program_id(1)
    nt = pl.num_programs(1) - 1
    tm = x_ref.shape[0]
    halo_w = 2 * CONV_WIDTH

    @pl.when(i == 0)
    def _():
        pprev[...] = jnp.zeros_like(pprev)
        tail[...] = jnp.zeros_like(tail)

    x = x_ref[...]
    ms = jnp.mean(x * x, axis=-1, keepdims=True)
    h = (x * lax.rsqrt(ms + EPS) * g_ref[...]).astype(BF16)
    nxt = jnp.dot(h[0:CONV_HALO, :], w_ref[:, 0:halo_w], preferred_element_type=F32).astype(P_DTYPE)
    tn = 512

    def project(j):
        pj = jnp.dot(h, w_ref[:, j * tn:(j + 1) * tn], preferred_element_type=F32).astype(P_DTYPE)
        if (j + 1) * tn <= FRONT_COLS:
            pcur[:, j * tn:(j + 1) * tn] = pj
        else:
            ph_ref[:, j * tn - FRONT_COLS:(j + 1) * tn - FRONT_COLS] = pj

    project(0)

    def glu(a, gt):
        return a.astype(F32) * jax.nn.sigmoid(gt.astype(F32))

    va, gl, gt = (slice(c * CONV_WIDTH, (c + 1) * CONV_WIDTH) for c in CONV_COLS)
    ybuf[0, CONV_HALO:CONV_HALO + tm, :] = glu(pprev[:, va], pprev[:, gl])
    ybuf[0, 0:CONV_HALO, :] = jnp.where(i > 1, glu(tail[:, va], tail[:, gl]), 0.0)
    ybuf[0, CONV_HALO + tm:, :] = jnp.where(i < nt, glu(nxt[:, va], nxt[:, gl]), 0.0)
    n_sh = tm + 2 * CONV_HALO - SUBLANES
    for r in range(1, SUBLANES):
        ybuf[r, 0:n_sh, :] = ybuf[0, r:r + n_sh, :]

    base = CONV_HALO - CONV_PAD
    n_conv = tm // CONV_ROWS
    for c in range(n_conv):
        if c % (n_conv // 4) == 0:
            project(1 + c // (n_conv // 4))
        r0 = c * CONV_ROWS
        acc = jnp.broadcast_to(cb_ref[...], (CONV_ROWS, CONV_WIDTH))
        for j in range(CONV_KERNEL):
            m, r = divmod(base + j, SUBLANES)
            start = r0 + m * SUBLANES
            acc = acc + cw_ref[j:j + 1, :] * ybuf[r, start:start + CONV_ROWS, :]
        cbuf[r0:r0 + CONV_ROWS, :] = acc

    project(5)
    z = jax.nn.silu(_layernorm_rows(cbuf[...], clg_ref[...], clb_ref[...]))
    z = jnp.dot(z.astype(BF16), pw_ref[...], preferred_element_type=F32)
    ya_ref[...] = (z * jax.nn.silu(pprev[:, gt].astype(F32))).astype(ya_ref.dtype)

    cu, cv, cg = (slice(c * SGU_WIDTH, (c + 1) * SGU_WIDTH) for c in SGU_COLS)
    col = lax.broadcasted_iota(jnp.int32, (SGU_CHUNK, SGU_WIDTH), 1)
    for c in range(tm // SGU_CHUNK):
        if c in (1, 3):
            project(6 + c // 2)
        rows = slice(c * SGU_CHUNK, (c + 1) * SGU_CHUNK)
        v = _gelu_exact(pprev[rows, cv].astype(F32))
        v = _layernorm_rows(v, slg_ref[...], slb_ref[...])
        stacked = jnp.concatenate(
            [jnp.where((col >= hd * SGU_HEAD_DIM) & (col < (hd + 1) * SGU_HEAD_DIM), v, 0.0)
             for hd in range(SGU_HEADS)], axis=0).astype(BF16)
        s = jnp.dot(sw_ref[...], stacked, preferred_element_type=F32) + sbias_ref[...]
        u = _gelu_exact(pprev[rows, cu].astype(F32))
        yb_ref[rows, :] = (u * s * jax.nn.silu(pprev[rows, cg].astype(F32))).astype(yb_ref.dtype)

    tail[...] = pprev[tm - CONV_HALO:tm, 0:halo_w]
    pprev[...] = pcur[...]


def _front(x, g, w_bf16, cw, cb, clg, clb, pw_bf16, slg, slb, sw_cat_bf16, sbias):
    b, s, _ = x.shape
    tm = TOKEN_TILE
    nt = s // tm

    def whole(shape):
        return pl.BlockSpec(shape, lambda bi, i: (0,) * len(shape))

    def tile(width, lag):
        if lag:
            return pl.BlockSpec((None, tm, width), lambda bi, i: (bi, jnp.maximum(i - 1, 0), 0))
        return pl.BlockSpec((None, tm, width), lambda bi, i: (bi, jnp.minimum(i, nt - 1), 0))

    return pl.pallas_call(
        _front_kernel,
        grid=(b, nt + 1),
        in_specs=[tile(D_MODEL, 0), whole((1, D_MODEL)), whole((D_MODEL, D_IN)),
                  whole((CONV_KERNEL, CONV_WIDTH)), whole((1, CONV_WIDTH)), whole((1, CONV_WIDTH)),
                  whole((1, CONV_WIDTH)), whole((CONV_WIDTH, CONV_WIDTH)),
                  whole((1, SGU_WIDTH)), whole((1, SGU_WIDTH)),
                  whole((SGU_CHUNK, SGU_HEADS * SGU_CHUNK)), whole((SGU_CHUNK, SGU_WIDTH))],
        out_specs=[tile(HGRN_IN, 0), tile(CONV_WIDTH, 1), tile(SGU_WIDTH, 1)],
        out_shape=[jax.ShapeDtypeStruct((b, s, HGRN_IN), P_DTYPE),
                   jax.ShapeDtypeStruct((b, s, CONV_WIDTH), Y_DTYPE),
                   jax.ShapeDtypeStruct((b, s, SGU_WIDTH), Y_DTYPE)],
        scratch_shapes=[pltpu.VMEM((tm, FRONT_COLS), P_DTYPE),
                        pltpu.VMEM((tm, FRONT_COLS), P_DTYPE),
                        pltpu.VMEM((CONV_HALO, 2 * CONV_WIDTH), P_DTYPE),
                        pltpu.VMEM((SUBLANES, tm + 2 * CONV_HALO, CONV_WIDTH), F32),
                        pltpu.VMEM((tm, CONV_WIDTH), F32)],
        compiler_params=_params(("arbitrary", "arbitrary")),
        name="front",
    )(x, g, w_bf16, cw, cb, clg, clb, pw_bf16, slg, slb, sw_cat_bf16, sbias)


def _lower_bound_row(lb_ref, layer):
    p = lb_ref[...].astype(F32)
    m = jnp.max(p, axis=0, keepdims=True)
    e = jnp.exp(p - m)
    sm = e / jnp.sum(e, axis=0, keepdims=True)
    lb = jnp.zeros((1, p.shape[1]), F32)
    for j in range(1, layer + 1):
        lb = lb + sm[j:j + 1, :]
    return lb


def _decay_and_key(z, lb):
    f = lb + (1.0 - lb) * (1.0 / (1.0 + jnp.exp(-z)))
    return f, 1.0 - f


def _inblock_scan(f, row, forward):
    half = HGRN_BLOCK // 2
    a, b, t = list(f), [jnp.ones_like(f[0]), jnp.ones_like(f[1])], list(f)
    per_level = []
    h = 1
    while h < half:
        bit = (row & h) != 0
        query_half = bit if forward else jnp.logical_not(bit)
        per_level.append((tuple(a), tuple(b)))
        for s in range(2):
            partner = jnp.where(bit, pltpu.roll(t[s], h, 0), pltpu.roll(t[s], half - h, 0))
            a[s] = a[s] * jnp.where(query_half, partner, 1.0)
            b[s] = b[s] * jnp.where(query_half, 1.0, partner)
            t[s] = t[s] * partner
        h *= 2
    per_level.append((tuple(a), tuple(b)))
    if forward:
        a[1] = a[1] * t[0]
        b[0] = b[0] * t[1]
    else:
        a[0] = a[0] * t[1]
        b[1] = b[1] * t[0]
    total = t[0] * t[1]
    return per_level, a, b, total


def _hgrn_kernel(q_ref, v_ref, zf_ref, zb_ref, gate_ref, lbf_ref, lbb_ref, ng_ref,
                 x_ref, ya_ref, yb_ref, wout_ref, fg_ref, o_ref,
                 qlv_all, klv_all, abuf_all, sball, sf, sb, ycat_all, *, layer, final):
    ph = pl.program_id(1)
    i = pl.program_id(2)
    n = pl.num_programs(2)
    cs = HGRN_CHUNK
    w = HGRN_WIDTH
    half = HGRN_BLOCK // 2
    g = q_ref.shape[0] // cs
    nb = cs // HGRN_BLOCK
    nl = HGRN_LEVELS
    n_in = 4
    slot_diag, slot_f, slot_b = nl, nl + 1, nl + 2
    mix_c = CONV_WIDTH + SGU_WIDTH
    a_f, b_f, t_f, a_b, b_b, t_b, q32, k_f, k_b = range(9)

    lb_f = _lower_bound_row(lbf_ref, layer)
    lb_b = _lower_bound_row(lbb_ref, layer)
    row = lax.broadcasted_iota(jnp.int32, (half, w), 0)
    heads = [slice(hd * HGRN_DK, (hd + 1) * HGRN_DK) for hd in range(HGRN_HEADS)]

    def halves(x):
        return [x[0:half, :], x[half:, :]]

    def whole(lo_hi):
        return jnp.concatenate(list(lo_hi), axis=0)

    def pair_totals(abuf, planes, hb):
        for blk in range(nb):
            if not (blk & hb):
                r0 = pl.ds(blk * HGRN_BLOCK, HGRN_BLOCK)
                r1 = pl.ds((blk ^ hb) * HGRN_BLOCK, HGRN_BLOCK)
                for plane in planes:
                    prod = abuf[plane, r0, :] * abuf[plane, r1, :]
                    abuf[plane, r0, :] = prod
                    abuf[plane, r1, :] = prod

    def sweep0_chunk(base, c, buf):
        abuf = abuf_all.at[buf]
        sball[c] = sb[...].astype(BF16)

        for blk in range(nb):
            rl = pl.ds(blk * HGRN_BLOCK, HGRN_BLOCK)
            rg = pl.ds(base + blk * HGRN_BLOCK, HGRN_BLOCK)
            f, k = _decay_and_key(zb_ref[rg, :].astype(F32), lb_b)
            _, _, b, t = _inblock_scan(halves(f), row, forward=False)
            abuf[b_b, rl, :] = whole(b)
            abuf[t_b, rl, :] = whole((t, t))
            abuf[k_b, rl, :] = k

        hb = 1
        while hb < nb:
            for blk in range(nb):
                if blk & hb:
                    r = pl.ds(blk * HGRN_BLOCK, HGRN_BLOCK)
                    rp = pl.ds((blk ^ hb) * HGRN_BLOCK, HGRN_BLOCK)
                    abuf[b_b, r, :] = abuf[b_b, r, :] * abuf[t_b, rp, :]
            pair_totals(abuf, (t_b,), hb)
            hb *= 2

        kst = (abuf[k_b] * abuf[b_b]).astype(BF16)
        vb = v_ref[pl.ds(base, cs), :].astype(BF16)
        tot = abuf[t_b, 0:1, :]
        for hd in range(HGRN_HEADS):
            cols = slice(hd * HGRN_DK, (hd + 1) * HGRN_DK)
            upd_s = lax.dot_general(vb[:, cols], kst[:, cols], (((0,), (0,)), ((), ())),
                                    preferred_element_type=F32)
            sb[hd] = sb[hd] * tot[:, cols] + upd_s

    def sweep1_chunk(base, c, buf):
        qlv, klv, abuf, ycat = qlv_all.at[buf], klv_all.at[buf], abuf_all.at[buf], ycat_all.at[buf]
        for blk in range(nb):
            rl = pl.ds(blk * HGRN_BLOCK, HGRN_BLOCK)
            rg = pl.ds(base + blk * HGRN_BLOCK, HGRN_BLOCK)
            q16 = q_ref[rg, :].astype(F32)
            ff, kf = _decay_and_key(zf_ref[rg, :].astype(F32), lb_f)
            fb, kb = _decay_and_key(zb_ref[rg, :].astype(F32), lb_b)
            q, kf2, kb2 = halves(q16), halves(kf), halves(kb)
            lv_f, af, bf_, tf = _inblock_scan(halves(ff), row, forward=True)
            lv_b, ab, bb, tb = _inblock_scan(halves(fb), row, forward=False)
            for lvl in range(n_in - 1):
                bit = (row & (1 << lvl)) != 0
                (fa, fbk), (ba, bbk) = lv_f[lvl], lv_b[lvl]
                qo = [q[s] * jnp.where(bit, fa[s], ba[s]) for s in range(2)]
                ko = [jnp.where(bit, kb2[s] * bbk[s], kf2[s] * fbk[s]) for s in range(2)]
                qlv[lvl, rl, :] = whole(qo).astype(BF16)
                klv[lvl, rl, :] = whole(ko).astype(BF16)
            (fa, fbk), (ba, bbk) = lv_f[n_in - 1], lv_b[n_in - 1]
            qlv[n_in - 1, rl, :] = whole((q[0] * ba[0], q[1] * fa[1])).astype(BF16)
            klv[n_in - 1, rl, :] = whole((kf2[0] * fbk[0], kb2[1] * bbk[1])).astype(BF16)
            qlv[slot_diag, rl, :] = q16.astype(BF16)
            klv[slot_diag, rl, :] = (kf + kb).astype(BF16)
            for plane, val in ((a_f, whole(af)), (b_f, whole(bf_)), (t_f, whole((tf, tf))),
                               (a_b, whole(ab)), (b_b, whole(bb)), (t_b, whole((tb, tb))),
                               (q32, q16), (k_f, kf), (k_b, kb)):
                abuf[plane, rl, :] = val

        hb = 1
        lvl = n_in
        while hb < nb:
            for blk in range(nb):
                r = pl.ds(blk * HGRN_BLOCK, HGRN_BLOCK)
                rp = pl.ds((blk ^ hb) * HGRN_BLOCK, HGRN_BLOCK)
                if blk & hb:
                    qa, kb_ = abuf[a_f, r, :], abuf[b_b, r, :]
                    qlv[lvl, r, :] = (abuf[q32, r, :] * qa).astype(BF16)
                    klv[lvl, r, :] = (abuf[k_b, r, :] * kb_).astype(BF16)
                    abuf[a_f, r, :] = qa * abuf[t_f, rp, :]
                    abuf[b_b, r, :] = kb_ * abuf[t_b, rp, :]
                else:
                    qa, kb_ = abuf[a_b, r, :], abuf[b_f, r, :]
                    qlv[lvl, r, :] = (abuf[q32, r, :] * qa).astype(BF16)
                    klv[lvl, r, :] = (abuf[k_f, r, :] * kb_).astype(BF16)
                    abuf[a_b, r, :] = qa * abuf[t_b, rp, :]
                    abuf[b_f, r, :] = kb_ * abuf[t_f, rp, :]
            pair_totals(abuf, (t_f, t_b), hb)
            hb *= 2
            lvl += 1

        q_all = abuf[q32]
        qlv[slot_f] = (q_all * abuf[a_f]).astype(BF16)
        qlv[slot_b] = (q_all * abuf[a_b]).astype(BF16)
        klv[slot_f] = (abuf[k_f] * abuf[b_f]).astype(BF16)
        tot_f = abuf[t_f, 0:1, :]

        ti = lax.broadcasted_iota(jnp.int32, (cs, cs), 0)
        si = lax.broadcasted_iota(jnp.int32, (cs, cs), 1)
        x = ti ^ si
        level_id = jnp.where(x == 0, slot_diag, 31 - lax.clz(x))

        rows = pl.ds(base, cs)
        vb = v_ref[rows, :].astype(BF16)
        ng = ng_ref[...]
        nt_dims = (((1,), (1,)), ((), ()))
        scores = [None] * HGRN_HEADS
        for lvl in range(nl + 1):
            mask = level_id == lvl
            for hd, cols in enumerate(heads):
                s_l = lax.dot_general(qlv[lvl, :, cols], klv[lvl, :, cols], nt_dims,
                                      preferred_element_type=F32)
                scores[hd] = s_l if lvl == 0 else jnp.where(mask, s_l, scores[hd])
        for hd, cols in enumerate(heads):
            o = jnp.dot(scores[hd].astype(BF16), vb[:, cols], preferred_element_type=F32)
            q_in = jnp.concatenate([qlv[slot_f, :, cols], qlv[slot_b, :, cols]], axis=1)
            s_in = jnp.concatenate([sf[hd].astype(BF16), sball[c, hd]], axis=1)
            o = o + lax.dot_general(q_in, s_in, nt_dims, preferred_element_type=F32)
            o = o * lax.rsqrt(jnp.mean(o * o, axis=-1, keepdims=True) + EPS) * ng
            ycat[:, mix_c + hd * HGRN_DK:mix_c + (hd + 1) * HGRN_DK] = (
                o * jax.nn.silu(gate_ref[rows, cols].astype(F32))).astype(BF16)
            upd_s = lax.dot_general(vb[:, cols], klv[slot_f, :, cols], (((0,), (0,)), ((), ())),
                                    preferred_element_type=F32)
            sf[hd] = sf[hd] * tot_f[:, cols] + upd_s

        ycat[:, 0:CONV_WIDTH] = ya_ref[rows, :]
        ycat[:, CONV_WIDTH:mix_c] = yb_ref[rows, :]
        acc = x_ref[rows, :] + jnp.dot(ycat[...], wout_ref[...], preferred_element_type=F32)
        if final:
            ms = jnp.mean(acc * acc, axis=-1, keepdims=True)
            acc = acc * lax.rsqrt(ms + EPS) * fg_ref[...]
        o_ref[rows, :] = acc

    @pl.when(ph == 0)
    def _():
        @pl.when(i == 0)
        def _():
            sb[...] = jnp.zeros_like(sb)

        for j in reversed(range(g)):
            sweep0_chunk(j * cs, (n - 1 - i) * g + j, j % 2)

    @pl.when(ph == 1)
    def _():
        @pl.when(i == 0)
        def _():
            sf[...] = jnp.zeros_like(sf)

        for j in range(g):
            sweep1_chunk(j * cs, i * g + j, j % 2)


def _hgrn_outproj(p, lb_fwd, lb_bwd, norm_g, x, ya, yb, w_out_bf16, final_g, layer, final):
    b, s, _ = p.shape
    cs = HGRN_CHUNK
    rows = HGRN_CHUNKS_PER_STEP * cs
    n = s // rows
    cq, ci, cff, cfb, cg = HGRN_COLS

    def both(col):
        return pl.BlockSpec((None, rows, HGRN_WIDTH),
                            lambda bi, ph, i: (bi, jnp.where(ph == 0, n - 1 - i, i), col))

    def second(col, width=HGRN_WIDTH):
        return pl.BlockSpec((None, rows, width), lambda bi, ph, i: (bi, ph * i, col))

    def whole(shape):
        return pl.BlockSpec(shape, lambda bi, ph, i: (0,) * len(shape))

    n_slots = HGRN_LEVELS + 3
    return pl.pallas_call(
        functools.partial(_hgrn_kernel, layer=layer, final=final),
        grid=(b, 2, n),
        in_specs=[second(cq), both(ci), second(cff), both(cfb), second(cg),
                  whole((DEPTH, HGRN_WIDTH)), whole((DEPTH, HGRN_WIDTH)), whole((1, HGRN_DK)),
                  second(0, D_MODEL), second(0, CONV_WIDTH), second(0, SGU_WIDTH),
                  whole((D_MIX, D_MODEL)), whole((1, D_MODEL))],
        out_specs=second(0, D_MODEL),
        out_shape=jax.ShapeDtypeStruct((b, s, D_MODEL), F32),
        scratch_shapes=[
            pltpu.VMEM((2, n_slots, cs, HGRN_WIDTH), BF16),
            pltpu.VMEM((2, n_slots, cs, HGRN_WIDTH), BF16),
            pltpu.VMEM((2, 9, cs, HGRN_WIDTH), F32),
            pltpu.VMEM((s // cs, HGRN_HEADS, HGRN_DK, HGRN_DK), BF16),
            pltpu.VMEM((HGRN_HEADS, HGRN_DK, HGRN_DK), F32),
            pltpu.VMEM((HGRN_HEADS, HGRN_DK, HGRN_DK), F32),
            pltpu.VMEM((2, cs, D_MIX), BF16),
        ],
        compiler_params=_params(("arbitrary", "arbitrary", "arbitrary")),
        name="hgrn_branch",
    )(p, p, p, p, p, lb_fwd, lb_bwd, norm_g, x, ya, yb, w_out_bf16, final_g)


def _trunk(x, prm):
    b, s, d = x.shape
    for l in range(DEPTH):
        p, ya, yb = _front(x, prm["norm_g"][l], prm["w_in"][l], prm["conv_w"][l], prm["conv_b"][l],
                           prm["conv_ln_g"][l], prm["conv_ln_b"][l], prm["conv_pw"][l],
                           prm["sgu_ln_g"][l], prm["sgu_ln_b"][l], prm["sgu_w"][l], prm["sgu_b"][l])
        x = _hgrn_outproj(p, prm["hgrn_lb_fwd"], prm["hgrn_lb_bwd"], prm["hgrn_norm_g"][l], x, ya, yb,
                          prm["w_out"][l], prm["final_norm_g"], l, final=(l == DEPTH - 1))
    return x


def kernel(x_prompt, x_sample, norm_g, w_in, conv_w, conv_b, conv_ln_g, conv_ln_b, conv_pw,
           sgu_ln_g, sgu_ln_b, sgu_w, sgu_b, hgrn_lb_fwd, hgrn_lb_bwd, hgrn_norm_g,
           w_out, final_norm_g):
    row = lambda a: a[:, None, :]
    prm = {
        "norm_g": row(norm_g),
        "w_in": w_in.astype(BF16),
        "conv_w": conv_w,
        "conv_b": row(conv_b),
        "conv_ln_g": row(conv_ln_g),
        "conv_ln_b": row(conv_ln_b),
        "conv_pw": conv_pw.astype(BF16),
        "sgu_ln_g": row(sgu_ln_g),
        "sgu_ln_b": row(sgu_ln_b),
        "sgu_w": sgu_w.transpose(0, 2, 1, 3).reshape(DEPTH, SGU_CHUNK, SGU_HEADS * SGU_CHUNK).astype(BF16),
        "sgu_b": jnp.repeat(sgu_b.transpose(0, 2, 1), SGU_HEAD_DIM, axis=2),
        "hgrn_lb_fwd": hgrn_lb_fwd,
        "hgrn_lb_bwd": hgrn_lb_bwd,
        "hgrn_norm_g": row(hgrn_norm_g),
        "w_out": w_out.astype(BF16),
        "final_norm_g": final_norm_g[None, :],
    }
    return (_trunk(x_prompt, prm), _trunk(x_sample, prm))
```

```python
import functools

import jax
import jax.numpy as jnp
from jax import lax
from jax.experimental import pallas as pl
from jax.experimental.pallas import tpu as pltpu

F32 = jnp.float32
BF16 = jnp.bfloat16

D_MODEL = 1024
DEPTH = 2
CONV_WIDTH = 256
CONV_KERNEL = 31
CONV_PAD = CONV_KERNEL // 2
SGU_WIDTH = 256
SGU_HEADS = 4
SGU_HEAD_DIM = SGU_WIDTH // SGU_HEADS
SGU_CHUNK = 128
HGRN_WIDTH = 512
HGRN_HEADS = 4
HGRN_DK = 128
D_IN = 4096
D_MIX = 1024
EPS = 1e-6

CONV_COLS = (0, 1, 2)
SGU_COLS = (3, 4, 5)
HGRN_COLS = (0, 1, 2, 3, 4)
FRONT_COLS = 3 * CONV_WIDTH + 3 * SGU_WIDTH
HGRN_IN = D_IN - FRONT_COLS

P_DTYPE = BF16
Y_DTYPE = BF16

V7X_VMEM_LIMIT_BYTES = 56 * 1024 * 1024
SUBLANES = 8

TOKEN_TILE = 512
CONV_HALO = 16
CONV_ROWS = 32
HGRN_CHUNK = 128
HGRN_CHUNKS_PER_STEP = 4
HGRN_BLOCK = 16
HGRN_LEVELS = 7


def _params(sem):
    return pltpu.CompilerParams(dimension_semantics=sem, vmem_limit_bytes=V7X_VMEM_LIMIT_BYTES)


def _layernorm_rows(y, g, b):
    mu = jnp.mean(y, axis=-1, keepdims=True)
    yc = y - mu
    return yc * lax.rsqrt(jnp.mean(yc * yc, axis=-1, keepdims=True) + EPS) * g + b


def _gelu_exact(x):
    return 0.5 * x * (1.0 + lax.erf(x * (2.0 ** -0.5)))


def _front_kernel(x_ref, g_ref, w_ref, cw_ref, cb_ref, clg_ref, clb_ref, pw_ref,
                  slg_ref, slb_ref, sw_ref, sbias_ref, ph_ref, ya_ref, yb_ref,
                  pcur, pprev, tail, ybuf, cbuf, *, nt):
    i = pl.program_id(0)
    tm = x_ref.shape[0]
    halo_w = 2 * CONV_WIDTH
    pos = lax.rem(jnp.maximum(i - 1, 0), nt)

    @pl.when(i == 0)
    def _():
        pprev[...] = jnp.zeros_like(pprev)
        tail[...] = jnp.zeros_like(tail)

    x = x_ref[...]
    ms = jnp.mean(x * x, axis=-1, keepdims=True)
    h = (x * lax.rsqrt(ms + EPS) * g_ref[...]).astype(BF16)
    nxt = jnp.dot(h[0:CONV_HALO, :], w_ref[:, 0:halo_w], preferred_element_type=F32).astype(P_DTYPE)
    tn = 512

    def project(j):
        pj = jnp.dot(h, w_ref[:, j * tn:(j + 1) * tn], preferred_element_type=F32).astype(P_DTYPE)
        if (j + 1) * tn <= FRONT_COLS:
            pcur[:, j * tn:(j + 1) * tn] = pj
        else:
            ph_ref[:, j * tn - FRONT_COLS:(j + 1) * tn - FRONT_COLS] = pj

    project(0)

    def glu(a, gt):
        return a.astype(F32) * jax.nn.sigmoid(gt.astype(F32))

    va, gl, gt = (slice(c * CONV_WIDTH, (c + 1) * CONV_WIDTH) for c in CONV_COLS)
    ybuf[0, CONV_HALO:CONV_HALO + tm, :] = glu(pprev[:, va], pprev[:, gl])
    ybuf[0, 0:CONV_HALO, :] = jnp.where(pos > 0, glu(tail[:, va], tail[:, gl]), 0.0)
    ybuf[0, CONV_HALO + tm:, :] = jnp.where(pos < nt - 1, glu(nxt[:, va], nxt[:, gl]), 0.0)
    n_sh = tm + 2 * CONV_HALO - SUBLANES
    for r in range(1, SUBLANES):
        ybuf[r, 0:n_sh, :] = ybuf[0, r:r + n_sh, :]

    base = CONV_HALO - CONV_PAD
    n_conv = tm // CONV_ROWS
    for c in range(n_conv):
        if c % (n_conv // 4) == 0:
            project(1 + c // (n_conv // 4))
        r0 = c * CONV_ROWS
        acc = jnp.broadcast_to(cb_ref[...], (CONV_ROWS, CONV_WIDTH))
        for j in range(CONV_KERNEL):
            m, r = divmod(base + j, SUBLANES)
            start = r0 + m * SUBLANES
            acc = acc + cw_ref[j:j + 1, :] * ybuf[r, start:start + CONV_ROWS, :]
        cbuf[r0:r0 + CONV_ROWS, :] = acc

    project(5)
    z = jax.nn.silu(_layernorm_rows(cbuf[...], clg_ref[...], clb_ref[...]))
    z = jnp.dot(z.astype(BF16), pw_ref[...], preferred_element_type=F32)
    ya_ref[...] = (z * jax.nn.silu(pprev[:, gt].astype(F32))).astype(ya_ref.dtype)

    cu, cv, cg = (slice(c * SGU_WIDTH, (c + 1) * SGU_WIDTH) for c in SGU_COLS)
    col = lax.broadcasted_iota(jnp.int32, (SGU_CHUNK, SGU_WIDTH), 1)
    for c in range(tm // SGU_CHUNK):
        if c in (1, 3):
            project(6 + c // 2)
        rows = slice(c * SGU_CHUNK, (c + 1) * SGU_CHUNK)
        v = _gelu_exact(pprev[rows, cv].astype(F32))
        v = _layernorm_rows(v, slg_ref[...], slb_ref[...])
        stacked = jnp.concatenate(
            [jnp.where((col >= hd * SGU_HEAD_DIM) & (col < (hd + 1) * SGU_HEAD_DIM), v, 0.0)
             for hd in range(SGU_HEADS)], axis=0).astype(BF16)
        s = jnp.dot(sw_ref[...], stacked, preferred_element_type=F32) + sbias_ref[...]
        u = _gelu_exact(pprev[rows, cu].astype(F32))
        yb_ref[rows, :] = (u * s * jax.nn.silu(pprev[rows, cg].astype(F32))).astype(yb_ref.dtype)

    tail[...] = pprev[tm - CONV_HALO:tm, 0:halo_w]
    pprev[...] = pcur[...]


def _front(x, g, w_bf16, cw, cb, clg, clb, pw_bf16, slg, slb, sw_cat_bf16, sbias):
    b, s, _ = x.shape
    tm = TOKEN_TILE
    nt = s // tm

    def whole(shape):
        return pl.BlockSpec(shape, lambda i: (0,) * len(shape))

    def tile(width, lag):
        def index(i):
            t = jnp.maximum(i - 1, 0) if lag else jnp.minimum(i, b * nt - 1)
            return (t // nt, lax.rem(t, nt), 0)
        return pl.BlockSpec((None, tm, width), index)

    return pl.pallas_call(
        functools.partial(_front_kernel, nt=nt),
        grid=(b * nt + 1,),
        in_specs=[tile(D_MODEL, 0), whole((1, D_MODEL)), whole((D_MODEL, D_IN)),
                  whole((CONV_KERNEL, CONV_WIDTH)), whole((1, CONV_WIDTH)), whole((1, CONV_WIDTH)),
                  whole((1, CONV_WIDTH)), whole((CONV_WIDTH, CONV_WIDTH)),
                  whole((1, SGU_WIDTH)), whole((1, SGU_WIDTH)),
                  whole((SGU_CHUNK, SGU_HEADS * SGU_CHUNK)), whole((SGU_CHUNK, SGU_WIDTH))],
        out_specs=[tile(HGRN_IN, 0), tile(CONV_WIDTH, 1), tile(SGU_WIDTH, 1)],
        out_shape=[jax.ShapeDtypeStruct((b, s, HGRN_IN), P_DTYPE),
                   jax.ShapeDtypeStruct((b, s, CONV_WIDTH), Y_DTYPE),
                   jax.ShapeDtypeStruct((b, s, SGU_WIDTH), Y_DTYPE)],
        scratch_shapes=[pltpu.VMEM((tm, FRONT_COLS), P_DTYPE),
                        pltpu.VMEM((tm, FRONT_COLS), P_DTYPE),
                        pltpu.VMEM((CONV_HALO, 2 * CONV_WIDTH), P_DTYPE),
                        pltpu.VMEM((SUBLANES, tm + 2 * CONV_HALO, CONV_WIDTH), F32),
                        pltpu.VMEM((tm, CONV_WIDTH), F32)],
        compiler_params=_params(("arbitrary",)),
        name="front",
    )(x, g, w_bf16, cw, cb, clg, clb, pw_bf16, slg, slb, sw_cat_bf16, sbias)


def _lower_bound_row(lb_ref, layer):
    p = lb_ref[...].astype(F32)
    m = jnp.max(p, axis=0, keepdims=True)
    e = jnp.exp(p - m)
    sm = e / jnp.sum(e, axis=0, keepdims=True)
    lb = jnp.zeros((1, p.shape[1]), F32)
    for j in range(1, layer + 1):
        lb = lb + sm[j:j + 1, :]
    return lb


def _decay_and_key(z, lb):
    f = lb + (1.0 - lb) * (1.0 / (1.0 + jnp.exp(-z)))
    return f, 1.0 - f


def _inblock_scan(f, row, forward):
    half = HGRN_BLOCK // 2
    a, b, t = list(f), [jnp.ones_like(f[0]), jnp.ones_like(f[1])], list(f)
    per_level = []
    h = 1
    while h < half:
        bit = (row & h) != 0
        query_half = bit if forward else jnp.logical_not(bit)
        per_level.append((tuple(a), tuple(b)))
        for s in range(2):
            partner = jnp.where(bit, pltpu.roll(t[s], h, 0), pltpu.roll(t[s], half - h, 0))
            a[s] = a[s] * jnp.where(query_half, partner, 1.0)
            b[s] = b[s] * jnp.where(query_half, 1.0, partner)
            t[s] = t[s] * partner
        h *= 2
    per_level.append((tuple(a), tuple(b)))
    if forward:
        a[1] = a[1] * t[0]
        b[0] = b[0] * t[1]
    else:
        a[0] = a[0] * t[1]
        b[1] = b[1] * t[0]
    total = t[0] * t[1]
    return per_level, a, b, total


def _hgrn_kernel(q_ref, v_ref, zf_ref, zb_ref, gate_ref, lbf_ref, lbb_ref, ng_ref,
                 x_ref, ya_ref, yb_ref, wout_ref, fg_ref, o_ref,
                 qlv_all, klv_all, abuf_all, sball, sf, sb, ycat_all, *, layer, final):
    ph = pl.program_id(1)
    i = pl.program_id(2)
    n = pl.num_programs(2)
    cs = HGRN_CHUNK
    w = HGRN_WIDTH
    half = HGRN_BLOCK // 2
    g = q_ref.shape[0] // cs
    nb = cs // HGRN_BLOCK
    nl = HGRN_LEVELS
    n_in = 4
    slot_diag, slot_f, slot_b = nl, nl + 1, nl + 2
    mix_c = CONV_WIDTH + SGU_WIDTH
    a_f, b_f, t_f, a_b, b_b, t_b, q32, k_f, k_b = range(9)

    lb_f = _lower_bound_row(lbf_ref, layer)
    lb_b = _lower_bound_row(lbb_ref, layer)
    row = lax.broadcasted_iota(jnp.int32, (half, w), 0)
    heads = [slice(hd * HGRN_DK, (hd + 1) * HGRN_DK) for hd in range(HGRN_HEADS)]

    def halves(x):
        return [x[0:half, :], x[half:, :]]

    def whole(lo_hi):
        return jnp.concatenate(list(lo_hi), axis=0)

    def pair_totals(abuf, planes, hb):
        for blk in range(nb):
            if not (blk & hb):
                r0 = pl.ds(blk * HGRN_BLOCK, HGRN_BLOCK)
                r1 = pl.ds((blk ^ hb) * HGRN_BLOCK, HGRN_BLOCK)
                for plane in planes:
                    prod = abuf[plane, r0, :] * abuf[plane, r1, :]
                    abuf[plane, r0, :] = prod
                    abuf[plane, r1, :] = prod

    def sweep0_chunk(base, c, buf):
        abuf = abuf_all.at[buf]
        sball[c] = sb[...].astype(BF16)

        for blk in range(nb):
            rl = pl.ds(blk * HGRN_BLOCK, HGRN_BLOCK)
            rg = pl.ds(base + blk * HGRN_BLOCK, HGRN_BLOCK)
            f, k = _decay_and_key(zb_ref[rg, :].astype(F32), lb_b)
            _, _, b, t = _inblock_scan(halves(f), row, forward=False)
            abuf[b_b, rl, :] = whole(b)
            abuf[t_b, rl, :] = whole((t, t))
            abuf[k_b, rl, :] = k

        hb = 1
        while hb < nb:
            for blk in range(nb):
                if blk & hb:
                    r = pl.ds(blk * HGRN_BLOCK, HGRN_BLOCK)
                    rp = pl.ds((blk ^ hb) * HGRN_BLOCK, HGRN_BLOCK)
                    abuf[b_b, r, :] = abuf[b_b, r, :] * abuf[t_b, rp, :]
            pair_totals(abuf, (t_b,), hb)
            hb *= 2

        kst = (abuf[k_b] * abuf[b_b]).astype(BF16)
        vb = v_ref[pl.ds(base, cs), :].astype(BF16)
        tot = abuf[t_b, 0:1, :]
        for hd in range(HGRN_HEADS):
            cols = slice(hd * HGRN_DK, (hd + 1) * HGRN_DK)
            upd_s = lax.dot_general(vb[:, cols], kst[:, cols], (((0,), (0,)), ((), ())),
                                    preferred_element_type=F32)
            sb[hd] = sb[hd] * tot[:, cols] + upd_s

    def sweep1_chunk(base, c, buf):
        qlv, klv, abuf, ycat = qlv_all.at[buf], klv_all.at[buf], abuf_all.at[buf], ycat_all.at[buf]
        for blk in range(nb):
            rl = pl.ds(blk * HGRN_BLOCK, HGRN_BLOCK)
            rg = pl.ds(base + blk * HGRN_BLOCK, HGRN_BLOCK)
            q16 = q_ref[rg, :].astype(F32)
            ff, kf = _decay_and_key(zf_ref[rg, :].astype(F32), lb_f)
            fb, kb = _decay_and_key(zb_ref[rg, :].astype(F32), lb_b)
            q, kf2, kb2 = halves(q16), halves(kf), halves(kb)
            lv_f, af, bf_, tf = _inblock_scan(halves(ff), row, forward=True)
            lv_b, ab, bb, tb = _inblock_scan(halves(fb), row, forward=False)
            for lvl in range(n_in - 1):
                bit = (row & (1 << lvl)) != 0
                (fa, fbk), (ba, bbk) = lv_f[lvl], lv_b[lvl]
                qo = [q[s] * jnp.where(bit, fa[s], ba[s]) for s in range(2)]
                ko = [jnp.where(bit, kb2[s] * bbk[s], kf2[s] * fbk[s]) for s in range(2)]
                qlv[lvl, rl, :] = whole(qo).astype(BF16)
                klv[lvl, rl, :] = whole(ko).astype(BF16)
            (fa, fbk), (ba, bbk) = lv_f[n_in - 1], lv_b[n_in - 1]
            qlv[n_in - 1, rl, :] = whole((q[0] * ba[0], q[1] * fa[1])).astype(BF16)
            klv[n_in - 1, rl, :] = whole((kf2[0] * fbk[0], kb2[1] * bbk[1])).astype(BF16)
            qlv[slot_diag, rl, :] = q16.astype(BF16)
            klv[slot_diag, rl, :] = (kf + kb).astype(BF16)
            for plane, val in ((a_f, whole(af)), (b_f, whole(bf_)), (t_f, whole((tf, tf))),
                               (a_b, whole(ab)), (b_b, whole(bb)), (t_b, whole((tb, tb))),
                               (q32, q16), (k_f, kf), (k_b, kb)):
                abuf[plane, rl, :] = val

        hb = 1
        lvl = n_in
        while hb < nb:
            for blk in range(nb):
                r = pl.ds(blk * HGRN_BLOCK, HGRN_BLOCK)
                rp = pl.ds((blk ^ hb) * HGRN_BLOCK, HGRN_BLOCK)
                if blk & hb:
                    qa, kb_ = abuf[a_f, r, :], abuf[b_b, r, :]
                    qlv[lvl, r, :] = (abuf[q32, r, :] * qa).astype(BF16)
                    klv[lvl, r, :] = (abuf[k_b, r, :] * kb_).astype(BF16)
                    abuf[a_f, r, :] = qa * abuf[t_f, rp, :]
                    abuf[b_b, r, :] = kb_ * abuf[t_b, rp, :]
                else:
                    qa, kb_ = abuf[a_b, r, :], abuf[b_f, r, :]
                    qlv[lvl, r, :] = (abuf[q32, r, :] * qa).astype(BF16)
                    klv[lvl, r, :] = (abuf[k_f, r, :] * kb_).astype(BF16)
                    abuf[a_b, r, :] = qa * abuf[t_b, rp, :]
                    abuf[b_f, r, :] = kb_ * abuf[t_f, rp, :]
            pair_totals(abuf, (t_f, t_b), hb)
            hb *= 2
            lvl += 1

        q_all = abuf[q32]
        qlv[slot_f] = (q_all * abuf[a_f]).astype(BF16)
        qlv[slot_b] = (q_all * abuf[a_b]).astype(BF16)
        klv[slot_f] = (abuf[k_f] * abuf[b_f]).astype(BF16)
        tot_f = abuf[t_f, 0:1, :]

        ti = lax.broadcasted_iota(jnp.int32, (cs, cs), 0)
        si = lax.broadcasted_iota(jnp.int32, (cs, cs), 1)
        x = ti ^ si
        level_id = jnp.where(x == 0, slot_diag, 31 - lax.clz(x))

        rows = pl.ds(base, cs)
        vb = v_ref[rows, :].astype(BF16)
        ng = ng_ref[...]
        nt_dims = (((1,), (1,)), ((), ()))
        scores = [None] * HGRN_HEADS
        for lvl in range(nl + 1):
            mask = level_id == lvl
            for hd, cols in enumerate(heads):
                s_l = lax.dot_general(qlv[lvl, :, cols], klv[lvl, :, cols], nt_dims,
                                      preferred_element_type=F32).astype(BF16)
                scores[hd] = s_l if lvl == 0 else jnp.where(mask, s_l, scores[hd])
        for hd, cols in enumerate(heads):
            o = jnp.dot(scores[hd], vb[:, cols], preferred_element_type=F32)
            q_in = jnp.concatenate([qlv[slot_f, :, cols], qlv[slot_b, :, cols]], axis=1)
            s_in = jnp.concatenate([sf[hd].astype(BF16), sball[c, hd]], axis=1)
            o = o + lax.dot_general(q_in, s_in, nt_dims, preferred_element_type=F32)
            o = o * lax.rsqrt(jnp.mean(o * o, axis=-1, keepdims=True) + EPS) * ng
            ycat[:, mix_c + hd * HGRN_DK:mix_c + (hd + 1) * HGRN_DK] = (
                o * jax.nn.silu(gate_ref[rows, cols].astype(F32))).astype(BF16)
            upd_s = lax.dot_general(vb[:, cols], klv[slot_f, :, cols], (((0,), (0,)), ((), ())),
                                    preferred_element_type=F32)
            sf[hd] = sf[hd] * tot_f[:, cols] + upd_s

        ycat[:, 0:CONV_WIDTH] = ya_ref[rows, :]
        ycat[:, CONV_WIDTH:mix_c] = yb_ref[rows, :]
        acc = x_ref[rows, :] + jnp.dot(ycat[...], wout_ref[...], preferred_element_type=F32)
        if final:
            ms = jnp.mean(acc * acc, axis=-1, keepdims=True)
            acc = acc * lax.rsqrt(ms + EPS) * fg_ref[...]
        o_ref[rows, :] = acc

    @pl.when(ph == 0)
    def _():
        @pl.when(i == 0)
        def _():
            sb[...] = jnp.zeros_like(sb)

        for j in reversed(range(g)):
            sweep0_chunk(j * cs, (n - 1 - i) * g + j, j % 2)

    @pl.when(ph == 1)
    def _():
        @pl.when(i == 0)
        def _():
            sf[...] = jnp.zeros_like(sf)

        for j in range(g):
            sweep1_chunk(j * cs, i * g + j, j % 2)


def _hgrn_outproj(p, lb_fwd, lb_bwd, norm_g, x, ya, yb, w_out_bf16, final_g, layer, final):
    b, s, _ = p.shape
    cs = HGRN_CHUNK
    rows = HGRN_CHUNKS_PER_STEP * cs
    n = s // rows
    cq, ci, cff, cfb, cg = HGRN_COLS

    def both(col):
        return pl.BlockSpec((None, rows, HGRN_WIDTH),
                            lambda bi, ph, i: (bi, jnp.where(ph == 0, n - 1 - i, i), col))

    def second(col, width=HGRN_WIDTH):
        return pl.BlockSpec((None, rows, width), lambda bi, ph, i: (bi, ph * i, col))

    def whole(shape):
        return pl.BlockSpec(shape, lambda bi, ph, i: (0,) * len(shape))

    n_slots = HGRN_LEVELS + 3
    return pl.pallas_call(
        functools.partial(_hgrn_kernel, layer=layer, final=final),
        grid=(b, 2, n),
        in_specs=[second(cq), both(ci), second(cff), both(cfb), second(cg),
                  whole((DEPTH, HGRN_WIDTH)), whole((DEPTH, HGRN_WIDTH)), whole((1, HGRN_DK)),
                  second(0, D_MODEL), second(0, CONV_WIDTH), second(0, SGU_WIDTH),
                  whole((D_MIX, D_MODEL)), whole((1, D_MODEL))],
        out_specs=second(0, D_MODEL),
        out_shape=jax.ShapeDtypeStruct((b, s, D_MODEL), F32),
        scratch_shapes=[
            pltpu.VMEM((2, n_slots, cs, HGRN_WIDTH), BF16),
            pltpu.VMEM((2, n_slots, cs, HGRN_WIDTH), BF16),
            pltpu.VMEM((2, 9, cs, HGRN_WIDTH), F32),
            pltpu.VMEM((s // cs, HGRN_HEADS, HGRN_DK, HGRN_DK), BF16),
            pltpu.VMEM((HGRN_HEADS, HGRN_DK, HGRN_DK), F32),
            pltpu.VMEM((HGRN_HEADS, HGRN_DK, HGRN_DK), F32),
            pltpu.VMEM((2, cs, D_MIX), BF16),
        ],
        compiler_params=_params(("arbitrary", "arbitrary", "arbitrary")),
        name="hgrn_branch",
    )(p, p, p, p, p, lb_fwd, lb_bwd, norm_g, x, ya, yb, w_out_bf16, final_g)


def _trunk(x, prm):
    for l in range(DEPTH):
        p, ya, yb = _front(x, prm["norm_g"][l], prm["w_in"][l], prm["conv_w"][l], prm["conv_b"][l],
                           prm["conv_ln_g"][l], prm["conv_ln_b"][l], prm["conv_pw"][l],
                           prm["sgu_ln_g"][l], prm["sgu_ln_b"][l], prm["sgu_w"][l], prm["sgu_b"][l])
        x = _hgrn_outproj(p, prm["hgrn_lb_fwd"], prm["hgrn_lb_bwd"], prm["hgrn_norm_g"][l], x, ya, yb,
                          prm["w_out"][l], prm["final_norm_g"], l, final=(l == DEPTH - 1))
    return x


def kernel(x_prompt, x_sample, norm_g, w_in, conv_w, conv_b, conv_ln_g, conv_ln_b, conv_pw,
           sgu_ln_g, sgu_ln_b, sgu_w, sgu_b, hgrn_lb_fwd, hgrn_lb_bwd, hgrn_norm_g,
           w_out, final_norm_g):
    row = lambda a: a[:, None, :]
    prm = {
        "norm_g": row(norm_g),
        "w_in": w_in.astype(BF16),
        "conv_w": conv_w,
        "conv_b": row(conv_b),
        "conv_ln_g": row(conv_ln_g),
        "conv_ln_b": row(conv_ln_b),
        "conv_pw": conv_pw.astype(BF16),
        "sgu_ln_g": row(sgu_ln_g),
        "sgu_ln_b": row(sgu_ln_b),
        "sgu_w": sgu_w.transpose(0, 2, 1, 3).reshape(DEPTH, SGU_CHUNK, SGU_HEADS * SGU_CHUNK).astype(BF16),
        "sgu_b": jnp.repeat(sgu_b.transpose(0, 2, 1), SGU_HEAD_DIM, axis=2),
        "hgrn_lb_fwd": hgrn_lb_fwd,
        "hgrn_lb_bwd": hgrn_lb_bwd,
        "hgrn_norm_g": row(hgrn_norm_g),
        "w_out": w_out.astype(BF16),
        "final_norm_g": final_norm_g[None, :],
    }
    return (_trunk(x_prompt, prm), _trunk(x_sample, prm))
```

```python
import functools

import jax
import jax.numpy as jnp
from jax import lax
from jax.experimental import pallas as pl
from jax.experimental.pallas import tpu as pltpu

F32 = jnp.float32
BF16 = jnp.bfloat16

D_MODEL = 1024
DEPTH = 2
CONV_WIDTH = 256
CONV_KERNEL = 31
CONV_PAD = CONV_KERNEL // 2
SGU_WIDTH = 256
SGU_HEADS = 4
SGU_HEAD_DIM = SGU_WIDTH // SGU_HEADS
SGU_CHUNK = 128
HGRN_WIDTH = 512
HGRN_HEADS = 4
HGRN_DK = 128
D_IN = 4096
D_MIX = 1024
EPS = 1e-6

CONV_COLS = (0, 1, 2)
SGU_COLS = (3, 4, 5)
HGRN_COLS = (0, 1, 2, 3, 4)
FRONT_COLS = 3 * CONV_WIDTH + 3 * SGU_WIDTH
HGRN_IN = D_IN - FRONT_COLS

P_DTYPE = BF16
Y_DTYPE = BF16

V7X_VMEM_LIMIT_BYTES = 56 * 1024 * 1024
SUBLANES = 8

TOKEN_TILE = 512
CONV_HALO = 16
CONV_ROWS = 32
HGRN_CHUNK = 128
HGRN_CHUNKS_PER_STEP = 4
HGRN_BLOCK = 16
HGRN_LEVELS = 7


def _params(sem):
    return pltpu.CompilerParams(dimension_semantics=sem, vmem_limit_bytes=V7X_VMEM_LIMIT_BYTES)


def _layernorm_rows(y, g, b):
    mu = jnp.mean(y, axis=-1, keepdims=True)
    yc = y - mu
    return yc * lax.rsqrt(jnp.mean(yc * yc, axis=-1, keepdims=True) + EPS) * g + b


def _gelu_exact(x):
    return 0.5 * x * (1.0 + lax.erf(x * (2.0 ** -0.5)))


def _front_kernel(x_ref, g_ref, w_ref, cw_ref, cb_ref, clg_ref, clb_ref, pw_ref,
                  slg_ref, slb_ref, sw_ref, sbias_ref, ph_ref, ya_ref, yb_ref,
                  pcur, pprev, tail, ybuf, cbuf, *, nt):
    i = pl.program_id(0)
    tm = x_ref.shape[0]
    halo_w = 2 * CONV_WIDTH
    pos = lax.rem(jnp.maximum(i - 1, 0), nt)

    @pl.when(i == 0)
    def _():
        pprev[...] = jnp.zeros_like(pprev)
        tail[...] = jnp.zeros_like(tail)

    x = x_ref[...]
    ms = jnp.mean(x * x, axis=-1, keepdims=True)
    h = (x * lax.rsqrt(ms + EPS) * g_ref[...]).astype(BF16)
    tn = 512

    def project(j):
        pj = jnp.dot(h, w_ref[:, j * tn:(j + 1) * tn], preferred_element_type=F32).astype(P_DTYPE)
        if (j + 1) * tn <= FRONT_COLS:
            pcur[:, j * tn:(j + 1) * tn] = pj
        else:
            ph_ref[:, j * tn - FRONT_COLS:(j + 1) * tn - FRONT_COLS] = pj

    project(0)
    nxt = pcur[0:CONV_HALO, 0:halo_w]

    def glu(a, gt):
        return a.astype(F32) * jax.nn.sigmoid(gt.astype(F32))

    va, gl, gt = (slice(c * CONV_WIDTH, (c + 1) * CONV_WIDTH) for c in CONV_COLS)
    ybuf[0, CONV_HALO:CONV_HALO + tm, :] = glu(pprev[:, va], pprev[:, gl])
    ybuf[0, 0:CONV_HALO, :] = jnp.where(pos > 0, glu(tail[:, va], tail[:, gl]), 0.0)
    ybuf[0, CONV_HALO + tm:, :] = jnp.where(pos < nt - 1, glu(nxt[:, va], nxt[:, gl]), 0.0)
    n_sh = tm + 2 * CONV_HALO - SUBLANES
    for r in range(1, SUBLANES):
        ybuf[r, 0:n_sh, :] = ybuf[0, r:r + n_sh, :]

    base = CONV_HALO - CONV_PAD
    n_conv = tm // CONV_ROWS
    for c in range(n_conv):
        if c % (n_conv // 4) == 0:
            project(1 + c // (n_conv // 4))
        r0 = c * CONV_ROWS
        acc = jnp.broadcast_to(cb_ref[...], (CONV_ROWS, CONV_WIDTH))
        for j in range(CONV_KERNEL):
            m, r = divmod(base + j, SUBLANES)
            start = r0 + m * SUBLANES
            acc = acc + cw_ref[j:j + 1, :] * ybuf[r, start:start + CONV_ROWS, :]
        cbuf[r0:r0 + CONV_ROWS, :] = acc

    project(5)
    z = jax.nn.silu(_layernorm_rows(cbuf[...], clg_ref[...], clb_ref[...]))
    z = jnp.dot(z.astype(BF16), pw_ref[...], preferred_element_type=F32)
    ya_ref[...] = (z * jax.nn.silu(pprev[:, gt].astype(F32))).astype(ya_ref.dtype)

    cu, cv, cg = (slice(c * SGU_WIDTH, (c + 1) * SGU_WIDTH) for c in SGU_COLS)
    col = lax.broadcasted_iota(jnp.int32, (SGU_CHUNK, SGU_WIDTH), 1)
    for c in range(tm // SGU_CHUNK):
        if c in (1, 3):
            project(6 + c // 2)
        rows = slice(c * SGU_CHUNK, (c + 1) * SGU_CHUNK)
        v = _gelu_exact(pprev[rows, cv].astype(F32))
        v = _layernorm_rows(v, slg_ref[...], slb_ref[...])
        stacked = jnp.concatenate(
            [jnp.where((col >= hd * SGU_HEAD_DIM) & (col < (hd + 1) * SGU_HEAD_DIM), v, 0.0)
             for hd in range(SGU_HEADS)], axis=0).astype(BF16)
        s = jnp.dot(sw_ref[...], stacked, preferred_element_type=F32) + sbias_ref[...]
        u = _gelu_exact(pprev[rows, cu].astype(F32))
        yb_ref[rows, :] = (u * s * jax.nn.silu(pprev[rows, cg].astype(F32))).astype(yb_ref.dtype)

    tail[...] = pprev[tm - CONV_HALO:tm, 0:halo_w]
    pprev[...] = pcur[...]


def _front(x, g, w_bf16, cw, cb, clg, clb, pw_bf16, slg, slb, sw_cat_bf16, sbias):
    b, s, _ = x.shape
    tm = TOKEN_TILE
    nt = s // tm

    def whole(shape):
        return pl.BlockSpec(shape, lambda i: (0,) * len(shape))

    def tile(width, lag):
        def index(i):
            t = jnp.maximum(i - 1, 0) if lag else jnp.minimum(i, b * nt - 1)
            return (t // nt, lax.rem(t, nt), 0)
        return pl.BlockSpec((None, tm, width), index)

    return pl.pallas_call(
        functools.partial(_front_kernel, nt=nt),
        grid=(b * nt + 1,),
        in_specs=[tile(D_MODEL, 0), whole((1, D_MODEL)), whole((D_MODEL, D_IN)),
                  whole((CONV_KERNEL, CONV_WIDTH)), whole((1, CONV_WIDTH)), whole((1, CONV_WIDTH)),
                  whole((1, CONV_WIDTH)), whole((CONV_WIDTH, CONV_WIDTH)),
                  whole((1, SGU_WIDTH)), whole((1, SGU_WIDTH)),
                  whole((SGU_CHUNK, SGU_HEADS * SGU_CHUNK)), whole((SGU_CHUNK, SGU_WIDTH))],
        out_specs=[tile(HGRN_IN, 0), tile(CONV_WIDTH, 1), tile(SGU_WIDTH, 1)],
        out_shape=[jax.ShapeDtypeStruct((b, s, HGRN_IN), P_DTYPE),
                   jax.ShapeDtypeStruct((b, s, CONV_WIDTH), Y_DTYPE),
                   jax.ShapeDtypeStruct((b, s, SGU_WIDTH), Y_DTYPE)],
        scratch_shapes=[pltpu.VMEM((tm, FRONT_COLS), P_DTYPE),
                        pltpu.VMEM((tm, FRONT_COLS), P_DTYPE),
                        pltpu.VMEM((CONV_HALO, 2 * CONV_WIDTH), P_DTYPE),
                        pltpu.VMEM((SUBLANES, tm + 2 * CONV_HALO, CONV_WIDTH), F32),
                        pltpu.VMEM((tm, CONV_WIDTH), F32)],
        compiler_params=_params(("arbitrary",)),
        name="front",
    )(x, g, w_bf16, cw, cb, clg, clb, pw_bf16, slg, slb, sw_cat_bf16, sbias)


def _lower_bound_row(lb_ref, layer):
    p = lb_ref[...].astype(F32)
    m = jnp.max(p, axis=0, keepdims=True)
    e = jnp.exp(p - m)
    sm = e / jnp.sum(e, axis=0, keepdims=True)
    lb = jnp.zeros((1, p.shape[1]), F32)
    for j in range(1, layer + 1):
        lb = lb + sm[j:j + 1, :]
    return lb


def _decay_and_key(z, lb):
    f = lb + (1.0 - lb) * (1.0 / (1.0 + jnp.exp(-z)))
    return f, 1.0 - f


def _inblock_scan(f, row, forward):
    half = HGRN_BLOCK // 2
    a, b, t = list(f), [jnp.ones_like(f[0]), jnp.ones_like(f[1])], list(f)
    per_level = []
    h = 1
    while h < half:
        bit = (row & h) != 0
        query_half = bit if forward else jnp.logical_not(bit)
        per_level.append((tuple(a), tuple(b)))
        for s in range(2):
            partner = jnp.where(bit, pltpu.roll(t[s], h, 0), pltpu.roll(t[s], half - h, 0))
            a[s] = a[s] * jnp.where(query_half, partner, 1.0)
            b[s] = b[s] * jnp.where(query_half, 1.0, partner)
            t[s] = t[s] * partner
        h *= 2
    per_level.append((tuple(a), tuple(b)))
    if forward:
        a[1] = a[1] * t[0]
        b[0] = b[0] * t[1]
    else:
        a[0] = a[0] * t[1]
        b[1] = b[1] * t[0]
    total = t[0] * t[1]
    return per_level, a, b, total


def _hgrn_kernel(q_ref, v_ref, zf_ref, zb_ref, gate_ref, lbf_ref, lbb_ref, ng_ref,
                 x_ref, ya_ref, yb_ref, wout_ref, fg_ref, o_ref,
                 qlv_all, klv_all, abuf_all, sball, sf, sb, ycat_all, *, layer, final):
    ph = pl.program_id(1)
    i = pl.program_id(2)
    n = pl.num_programs(2)
    cs = HGRN_CHUNK
    w = HGRN_WIDTH
    half = HGRN_BLOCK // 2
    g = q_ref.shape[0] // cs
    nb = cs // HGRN_BLOCK
    nl = HGRN_LEVELS
    n_in = 4
    slot_diag, slot_f, slot_b = nl, nl + 1, nl + 2
    mix_c = CONV_WIDTH + SGU_WIDTH
    a_f, b_f, t_f, a_b, b_b, t_b, q32, k_f, k_b = range(9)

    lb_f = _lower_bound_row(lbf_ref, layer)
    lb_b = _lower_bound_row(lbb_ref, layer)
    row = lax.broadcasted_iota(jnp.int32, (half, w), 0)
    heads = [slice(hd * HGRN_DK, (hd + 1) * HGRN_DK) for hd in range(HGRN_HEADS)]

    def halves(x):
        return [x[0:half, :], x[half:, :]]

    def whole(lo_hi):
        return jnp.concatenate(list(lo_hi), axis=0)

    def pair_totals(abuf, planes, hb):
        for blk in range(nb):
            if not (blk & hb):
                r0 = pl.ds(blk * HGRN_BLOCK, HGRN_BLOCK)
                r1 = pl.ds((blk ^ hb) * HGRN_BLOCK, HGRN_BLOCK)
                for plane in planes:
                    prod = abuf[plane, r0, :] * abuf[plane, r1, :]
                    abuf[plane, r0, :] = prod
                    abuf[plane, r1, :] = prod

    def sweep0_chunk(base, c, buf):
        abuf = abuf_all.at[buf]
        sball[c] = sb[...].astype(BF16)

        for blk in range(nb):
            rl = pl.ds(blk * HGRN_BLOCK, HGRN_BLOCK)
            rg = pl.ds(base + blk * HGRN_BLOCK, HGRN_BLOCK)
            f, k = _decay_and_key(zb_ref[rg, :].astype(F32), lb_b)
            _, _, b, t = _inblock_scan(halves(f), row, forward=False)
            abuf[b_b, rl, :] = whole(b)
            abuf[t_b, rl, :] = whole((t, t))
            abuf[k_b, rl, :] = k

        hb = 1
        while hb < nb:
            for blk in range(nb):
                if blk & hb:
                    r = pl.ds(blk * HGRN_BLOCK, HGRN_BLOCK)
                    rp = pl.ds((blk ^ hb) * HGRN_BLOCK, HGRN_BLOCK)
                    abuf[b_b, r, :] = abuf[b_b, r, :] * abuf[t_b, rp, :]
            pair_totals(abuf, (t_b,), hb)
            hb *= 2

        kst = (abuf[k_b] * abuf[b_b]).astype(BF16)
        vb = v_ref[pl.ds(base, cs), :].astype(BF16)
        tot = abuf[t_b, 0:1, :]
        for hd in range(HGRN_HEADS):
            cols = slice(hd * HGRN_DK, (hd + 1) * HGRN_DK)
            upd_s = lax.dot_general(vb[:, cols], kst[:, cols], (((0,), (0,)), ((), ())),
                                    preferred_element_type=F32)
            sb[hd] = sb[hd] * tot[:, cols] + upd_s

    def sweep1_chunk(base, c, buf):
        qlv, klv, abuf, ycat = qlv_all.at[buf], klv_all.at[buf], abuf_all.at[buf], ycat_all.at[buf]
        for blk in range(nb):
            rl = pl.ds(blk * HGRN_BLOCK, HGRN_BLOCK)
            rg = pl.ds(base + blk * HGRN_BLOCK, HGRN_BLOCK)
            q16 = q_ref[rg, :].astype(F32)
            ff, kf = _decay_and_key(zf_ref[rg, :].astype(F32), lb_f)
            fb, kb = _decay_and_key(zb_ref[rg, :].astype(F32), lb_b)
            q, kf2, kb2 = halves(q16), halves(kf), halves(kb)
            lv_f, af, bf_, tf = _inblock_scan(halves(ff), row, forward=True)
            lv_b, ab, bb, tb = _inblock_scan(halves(fb), row, forward=False)
            for lvl in range(n_in - 1):
                bit = (row & (1 << lvl)) != 0
                (fa, fbk), (ba, bbk) = lv_f[lvl], lv_b[lvl]
                qo = [q[s] * jnp.where(bit, fa[s], ba[s]) for s in range(2)]
                ko = [jnp.where(bit, kb2[s] * bbk[s], kf2[s] * fbk[s]) for s in range(2)]
                qlv[lvl, rl, :] = whole(qo).astype(BF16)
                klv[lvl, rl, :] = whole(ko).astype(BF16)
            (fa, fbk), (ba, bbk) = lv_f[n_in - 1], lv_b[n_in - 1]
            qlv[n_in - 1, rl, :] = whole((q[0] * ba[0], q[1] * fa[1])).astype(BF16)
            klv[n_in - 1, rl, :] = whole((kf2[0] * fbk[0], kb2[1] * bbk[1])).astype(BF16)
            qlv[slot_diag, rl, :] = q16.astype(BF16)
            klv[slot_diag, rl, :] = (kf + kb).astype(BF16)
            for plane, val in ((a_f, whole(af)), (b_f, whole(bf_)), (t_f, whole((tf, tf))),
                               (a_b, whole(ab)), (b_b, whole(bb)), (t_b, whole((tb, tb))),
                               (q32, q16), (k_f, kf), (k_b, kb)):
                abuf[plane, rl, :] = val

        hb = 1
        lvl = n_in
        while hb < nb:
            for blk in range(nb):
                r = pl.ds(blk * HGRN_BLOCK, HGRN_BLOCK)
                rp = pl.ds((blk ^ hb) * HGRN_BLOCK, HGRN_BLOCK)
                if blk & hb:
                    qa, kb_ = abuf[a_f, r, :], abuf[b_b, r, :]
                    qlv[lvl, r, :] = (abuf[q32, r, :] * qa).astype(BF16)
                    klv[lvl, r, :] = (abuf[k_b, r, :] * kb_).astype(BF16)
                    abuf[a_f, r, :] = qa * abuf[t_f, rp, :]
                    abuf[b_b, r, :] = kb_ * abuf[t_b, rp, :]
                else:
                    qa, kb_ = abuf[a_b, r, :], abuf[b_f, r, :]
                    qlv[lvl, r, :] = (abuf[q32, r, :] * qa).astype(BF16)
                    klv[lvl, r, :] = (abuf[k_f, r, :] * kb_).astype(BF16)
                    abuf[a_b, r, :] = qa * abuf[t_b, rp, :]
                    abuf[b_f, r, :] = kb_ * abuf[t_f, rp, :]
            pair_totals(abuf, (t_f, t_b), hb)
            hb *= 2
            lvl += 1

        q_all = abuf[q32]
        qlv[slot_f] = (q_all * abuf[a_f]).astype(BF16)
        qlv[slot_b] = (q_all * abuf[a_b]).astype(BF16)
        klv[slot_f] = (abuf[k_f] * abuf[b_f]).astype(BF16)
        tot_f = abuf[t_f, 0:1, :]

        ti = lax.broadcasted_iota(jnp.int32, (cs, cs), 0)
        si = lax.broadcasted_iota(jnp.int32, (cs, cs), 1)
        x = ti ^ si
        level_id = jnp.where(x == 0, slot_diag, 31 - lax.clz(x))

        rows = pl.ds(base, cs)
        vb = v_ref[rows, :].astype(BF16)
        ng = ng_ref[...]
        nt_dims = (((1,), (1,)), ((), ()))
        scores = [None] * HGRN_HEADS
        for lvl in range(nl + 1):
            mask = level_id == lvl
            for hd, cols in enumerate(heads):
                s_l = lax.dot_general(qlv[lvl, :, cols], klv[lvl, :, cols], nt_dims,
                                      preferred_element_type=F32).astype(BF16)
                scores[hd] = s_l if lvl == 0 else jnp.where(mask, s_l, scores[hd])
        for hd, cols in enumerate(heads):
            o = jnp.dot(scores[hd], vb[:, cols], preferred_element_type=F32)
            q_in = jnp.concatenate([qlv[slot_f, :, cols], qlv[slot_b, :, cols]], axis=1)
            s_in = jnp.concatenate([sf[hd].astype(BF16), sball[c, hd]], axis=1)
            o = o + lax.dot_general(q_in, s_in, nt_dims, preferred_element_type=F32)
            o = o * lax.rsqrt(jnp.mean(o * o, axis=-1, keepdims=True) + EPS) * ng
            ycat[:, mix_c + hd * HGRN_DK:mix_c + (hd + 1) * HGRN_DK] = (
                o * jax.nn.silu(gate_ref[rows, cols].astype(F32))).astype(BF16)
            upd_s = lax.dot_general(vb[:, cols], klv[slot_f, :, cols], (((0,), (0,)), ((), ())),
                                    preferred_element_type=F32)
            sf[hd] = sf[hd] * tot_f[:, cols] + upd_s

        ycat[:, 0:CONV_WIDTH] = ya_ref[rows, :]
        ycat[:, CONV_WIDTH:mix_c] = yb_ref[rows, :]
        acc = x_ref[rows, :] + jnp.dot(ycat[...], wout_ref[...], preferred_element_type=F32)
        if final:
            ms = jnp.mean(acc * acc, axis=-1, keepdims=True)
            acc = acc * lax.rsqrt(ms + EPS) * fg_ref[...]
        o_ref[rows, :] = acc

    @pl.when(ph == 0)
    def _():
        @pl.when(i == 0)
        def _():
            sb[...] = jnp.zeros_like(sb)

        for j in reversed(range(g)):
            sweep0_chunk(j * cs, (n - 1 - i) * g + j, j % 2)

    @pl.when(ph == 1)
    def _():
        @pl.when(i == 0)
        def _():
            sf[...] = jnp.zeros_like(sf)

        for j in range(g):
            sweep1_chunk(j * cs, i * g + j, j % 2)


def _hgrn_outproj(p, lb_fwd, lb_bwd, norm_g, x, ya, yb, w_out_bf16, final_g, layer, final):
    b, s, _ = p.shape
    cs = HGRN_CHUNK
    rows = HGRN_CHUNKS_PER_STEP * cs
    n = s // rows
    cq, ci, cff, cfb, cg = HGRN_COLS

    def both(col):
        return pl.BlockSpec((None, rows, HGRN_WIDTH),
                            lambda bi, ph, i: (bi, jnp.where(ph == 0, n - 1 - i, i), col))

    def second(col, width=HGRN_WIDTH):
        return pl.BlockSpec((None, rows, width), lambda bi, ph, i: (bi, ph * i, col))

    def whole(shape):
        return pl.BlockSpec(shape, lambda bi, ph, i: (0,) * len(shape))

    n_slots = HGRN_LEVELS + 3
    return pl.pallas_call(
        functools.partial(_hgrn_kernel, layer=layer, final=final),
        grid=(b, 2, n),
        in_specs=[second(cq), both(ci), second(cff), both(cfb), second(cg),
                  whole((DEPTH, HGRN_WIDTH)), whole((DEPTH, HGRN_WIDTH)), whole((1, HGRN_DK)),
                  second(0, D_MODEL), second(0, CONV_WIDTH), second(0, SGU_WIDTH),
                  whole((D_MIX, D_MODEL)), whole((1, D_MODEL))],
        out_specs=second(0, D_MODEL),
        out_shape=jax.ShapeDtypeStruct((b, s, D_MODEL), F32),
        scratch_shapes=[
            pltpu.VMEM((2, n_slots, cs, HGRN_WIDTH), BF16),
            pltpu.VMEM((2, n_slots, cs, HGRN_WIDTH), BF16),
            pltpu.VMEM((2, 9, cs, HGRN_WIDTH), F32),
            pltpu.VMEM((s // cs, HGRN_HEADS, HGRN_DK, HGRN_DK), BF16),
            pltpu.VMEM((HGRN_HEADS, HGRN_DK, HGRN_DK), F32),
            pltpu.VMEM((HGRN_HEADS, HGRN_DK, HGRN_DK), F32),
            pltpu.VMEM((2, cs, D_MIX), BF16),
        ],
        compiler_params=_params(("arbitrary", "arbitrary", "arbitrary")),
        name="hgrn_branch",
    )(p, p, p, p, p, lb_fwd, lb_bwd, norm_g, x, ya, yb, w_out_bf16, final_g)


def _trunk(x, prm):
    for l in range(DEPTH):
        p, ya, yb = _front(x, prm["norm_g"][l], prm["w_in"][l], prm["conv_w"][l], prm["conv_b"][l],
                           prm["conv_ln_g"][l], prm["conv_ln_b"][l], prm["conv_pw"][l],
                           prm["sgu_ln_g"][l], prm["sgu_ln_b"][l], prm["sgu_w"][l], prm["sgu_b"][l])
        x = _hgrn_outproj(p, prm["hgrn_lb_fwd"], prm["hgrn_lb_bwd"], prm["hgrn_norm_g"][l], x, ya, yb,
                          prm["w_out"][l], prm["final_norm_g"], l, final=(l == DEPTH - 1))
    return x


def kernel(x_prompt, x_sample, norm_g, w_in, conv_w, conv_b, conv_ln_g, conv_ln_b, conv_pw,
           sgu_ln_g, sgu_ln_b, sgu_w, sgu_b, hgrn_lb_fwd, hgrn_lb_bwd, hgrn_norm_g,
           w_out, final_norm_g):
    row = lambda a: a[:, None, :]
    prm = {
        "norm_g": row(norm_g),
        "w_in": w_in.astype(BF16),
        "conv_w": conv_w,
        "conv_b": row(conv_b),
        "conv_ln_g": row(conv_ln_g),
        "conv_ln_b": row(conv_ln_b),
        "conv_pw": conv_pw.astype(BF16),
        "sgu_ln_g": row(sgu_ln_g),
        "sgu_ln_b": row(sgu_ln_b),
        "sgu_w": sgu_w.transpose(0, 2, 1, 3).reshape(DEPTH, SGU_CHUNK, SGU_HEADS * SGU_CHUNK).astype(BF16),
        "sgu_b": jnp.repeat(sgu_b.transpose(0, 2, 1), SGU_HEAD_DIM, axis=2),
        "hgrn_lb_fwd": hgrn_lb_fwd,
        "hgrn_lb_bwd": hgrn_lb_bwd,
        "hgrn_norm_g": row(hgrn_norm_g),
        "w_out": w_out.astype(BF16),
        "final_norm_g": final_norm_g[None, :],
    }
    return (_trunk(x_prompt, prm), _trunk(x_sample, prm))
```

```python
import functools

import jax
import jax.numpy as jnp
from jax import lax
from jax.experimental import pallas as pl
from jax.experimental.pallas import tpu as pltpu

F32 = jnp.float32
BF16 = jnp.bfloat16

D_MODEL = 1024
DEPTH = 2
CONV_WIDTH = 256
CONV_KERNEL = 31
CONV_PAD = CONV_KERNEL // 2
SGU_WIDTH = 256
SGU_HEADS = 4
SGU_HEAD_DIM = SGU_WIDTH // SGU_HEADS
SGU_CHUNK = 128
HGRN_WIDTH = 512
HGRN_HEADS = 4
HGRN_DK = 128
D_IN = 4096
D_MIX = 1024
EPS = 1e-6

CONV_COLS = (0, 1, 2)
SGU_COLS = (3, 4, 5)
HGRN_COLS = (0, 1, 2, 3, 4)
FRONT_COLS = 3 * CONV_WIDTH + 3 * SGU_WIDTH
HGRN_IN = D_IN - FRONT_COLS

P_DTYPE = BF16
Y_DTYPE = BF16

V7X_VMEM_LIMIT_BYTES = 56 * 1024 * 1024
SUBLANES = 8

TOKEN_TILE = 512
CONV_HALO = 16
CONV_ROWS = 32
HGRN_CHUNK = 128
HGRN_CHUNKS_PER_STEP = 4
HGRN_BLOCK = 16
HGRN_LEVELS = 7


def _params(sem):
    return pltpu.CompilerParams(dimension_semantics=sem, vmem_limit_bytes=V7X_VMEM_LIMIT_BYTES)


def _layernorm_rows(y, g, b):
    mu = jnp.mean(y, axis=-1, keepdims=True)
    yc = y - mu
    return yc * lax.rsqrt(jnp.mean(yc * yc, axis=-1, keepdims=True) + EPS) * g + b


def _gelu_exact(x):
    return 0.5 * x * (1.0 + lax.erf(x * (2.0 ** -0.5)))


def _front_kernel(x_ref, g_ref, w_ref, cw_ref, cb_ref, clg_ref, clb_ref, pw_ref,
                  slg_ref, slb_ref, sw_ref, sbias_ref, ph_ref, ya_ref, yb_ref,
                  pcur, pprev, tail, ybuf, cbuf, *, nt):
    i = pl.program_id(0)
    tm = x_ref.shape[0]
    halo_w = 2 * CONV_WIDTH
    pos = lax.rem(jnp.maximum(i - 1, 0), nt)

    @pl.when(i == 0)
    def _():
        pprev[...] = jnp.zeros_like(pprev)
        tail[...] = jnp.zeros_like(tail)

    x = x_ref[...]
    ms = jnp.mean(x * x, axis=-1, keepdims=True)
    h = (x * lax.rsqrt(ms + EPS) * g_ref[...]).astype(BF16)
    nxt = jnp.dot(h[0:CONV_HALO, :], w_ref[:, 0:halo_w], preferred_element_type=F32).astype(P_DTYPE)
    tn = 512

    def project(j):
        pj = jnp.dot(h, w_ref[:, j * tn:(j + 1) * tn], preferred_element_type=F32).astype(P_DTYPE)
        if (j + 1) * tn <= FRONT_COLS:
            pcur[:, j * tn:(j + 1) * tn] = pj
        else:
            ph_ref[:, j * tn - FRONT_COLS:(j + 1) * tn - FRONT_COLS] = pj

    project(0)

    def glu(a, gt):
        return a.astype(F32) * jax.nn.sigmoid(gt.astype(F32))

    va, gl, gt = (slice(c * CONV_WIDTH, (c + 1) * CONV_WIDTH) for c in CONV_COLS)
    ybuf[0, CONV_HALO:CONV_HALO + tm, :] = glu(pprev[:, va], pprev[:, gl])
    ybuf[0, 0:CONV_HALO, :] = jnp.where(pos > 0, glu(tail[:, va], tail[:, gl]), 0.0)
    ybuf[0, CONV_HALO + tm:, :] = jnp.where(pos < nt - 1, glu(nxt[:, va], nxt[:, gl]), 0.0)
    n_sh = tm + 2 * CONV_HALO - SUBLANES
    for r in range(1, SUBLANES):
        ybuf[r, 0:n_sh, :] = ybuf[0, r:r + n_sh, :]

    base = CONV_HALO - CONV_PAD
    n_conv = tm // CONV_ROWS
    for c in range(n_conv):
        if c % (n_conv // 4) == 0:
            project(1 + c // (n_conv // 4))
        r0 = c * CONV_ROWS
        acc = jnp.broadcast_to(cb_ref[...], (CONV_ROWS, CONV_WIDTH))
        for j in range(CONV_KERNEL):
            m, r = divmod(base + j, SUBLANES)
            start = r0 + m * SUBLANES
            acc = acc + cw_ref[j:j + 1, :] * ybuf[r, start:start + CONV_ROWS, :]
        cbuf[r0:r0 + CONV_ROWS, :] = acc

    project(5)
    z = jax.nn.silu(_layernorm_rows(cbuf[...], clg_ref[...], clb_ref[...]))
    z = jnp.dot(z.astype(BF16), pw_ref[...], preferred_element_type=F32)
    ya_ref[...] = (z * jax.nn.silu(pprev[:, gt].astype(F32))).astype(ya_ref.dtype)

    cu, cv, cg = (slice(c * SGU_WIDTH, (c + 1) * SGU_WIDTH) for c in SGU_COLS)
    col = lax.broadcasted_iota(jnp.int32, (SGU_CHUNK, SGU_WIDTH), 1)
    for c in range(tm // SGU_CHUNK):
        if c in (1, 3):
            project(6 + c // 2)
        rows = slice(c * SGU_CHUNK, (c + 1) * SGU_CHUNK)
        v = _gelu_exact(pprev[rows, cv].astype(F32))
        v = _layernorm_rows(v, slg_ref[...], slb_ref[...])
        stacked = jnp.concatenate(
            [jnp.where((col >= hd * SGU_HEAD_DIM) & (col < (hd + 1) * SGU_HEAD_DIM), v, 0.0)
             for hd in range(SGU_HEADS)], axis=0).astype(BF16)
        s = jnp.dot(sw_ref[...], stacked, preferred_element_type=F32) + sbias_ref[...]
        u = _gelu_exact(pprev[rows, cu].astype(F32))
        yb_ref[rows, :] = (u * s * jax.nn.silu(pprev[rows, cg].astype(F32))).astype(yb_ref.dtype)

    tail[...] = pprev[tm - CONV_HALO:tm, 0:halo_w]
    pprev[...] = pcur[...]


def _front(x, g, w_bf16, cw, cb, clg, clb, pw_bf16, slg, slb, sw_cat_bf16, sbias):
    b, s, _ = x.shape
    tm = TOKEN_TILE
    nt = s // tm

    def whole(shape):
        return pl.BlockSpec(shape, lambda i: (0,) * len(shape))

    def tile(width, lag):
        def index(i):
            t = jnp.maximum(i - 1, 0) if lag else jnp.minimum(i, b * nt - 1)
            return (t // nt, lax.rem(t, nt), 0)
        return pl.BlockSpec((None, tm, width), index)

    return pl.pallas_call(
        functools.partial(_front_kernel, nt=nt),
        grid=(b * nt + 1,),
        in_specs=[tile(D_MODEL, 0), whole((1, D_MODEL)), whole((D_MODEL, D_IN)),
                  whole((CONV_KERNEL, CONV_WIDTH)), whole((1, CONV_WIDTH)), whole((1, CONV_WIDTH)),
                  whole((1, CONV_WIDTH)), whole((CONV_WIDTH, CONV_WIDTH)),
                  whole((1, SGU_WIDTH)), whole((1, SGU_WIDTH)),
                  whole((SGU_CHUNK, SGU_HEADS * SGU_CHUNK)), whole((SGU_CHUNK, SGU_WIDTH))],
        out_specs=[tile(HGRN_IN, 0), tile(CONV_WIDTH, 1), tile(SGU_WIDTH, 1)],
        out_shape=[jax.ShapeDtypeStruct((b, s, HGRN_IN), P_DTYPE),
                   jax.ShapeDtypeStruct((b, s, CONV_WIDTH), Y_DTYPE),
                   jax.ShapeDtypeStruct((b, s, SGU_WIDTH), Y_DTYPE)],
        scratch_shapes=[pltpu.VMEM((tm, FRONT_COLS), P_DTYPE),
                        pltpu.VMEM((tm, FRONT_COLS), P_DTYPE),
                        pltpu.VMEM((CONV_HALO, 2 * CONV_WIDTH), P_DTYPE),
                        pltpu.VMEM((SUBLANES, tm + 2 * CONV_HALO, CONV_WIDTH), F32),
                        pltpu.VMEM((tm, CONV_WIDTH), F32)],
        compiler_params=_params(("arbitrary",)),
        name="front",
    )(x, g, w_bf16, cw, cb, clg, clb, pw_bf16, slg, slb, sw_cat_bf16, sbias)


def _lower_bound_row(lb_ref, layer):
    p = lb_ref[...].astype(F32)
    m = jnp.max(p, axis=0, keepdims=True)
    e = jnp.exp(p - m)
    sm = e / jnp.sum(e, axis=0, keepdims=True)
    lb = jnp.zeros((1, p.shape[1]), F32)
    for j in range(1, layer + 1):
        lb = lb + sm[j:j + 1, :]
    return lb


def _decay_and_key(z, lb):
    f = lb + (1.0 - lb) * (1.0 / (1.0 + jnp.exp(-z)))
    return f, 1.0 - f


def _inblock_scan(f, row, forward):
    half = HGRN_BLOCK // 2
    a, b, t = list(f), [jnp.ones_like(f[0]), jnp.ones_like(f[1])], list(f)
    per_level = []
    h = 1
    while h < half:
        bit = (row & h) != 0
        query_half = bit if forward else jnp.logical_not(bit)
        per_level.append((tuple(a), tuple(b)))
        for s in range(2):
            partner = jnp.where(bit, pltpu.roll(t[s], h, 0), pltpu.roll(t[s], half - h, 0))
            a[s] = a[s] * jnp.where(query_half, partner, 1.0)
            b[s] = b[s] * jnp.where(query_half, 1.0, partner)
            t[s] = t[s] * partner
        h *= 2
    per_level.append((tuple(a), tuple(b)))
    if forward:
        a[1] = a[1] * t[0]
        b[0] = b[0] * t[1]
    else:
        a[0] = a[0] * t[1]
        b[1] = b[1] * t[0]
    total = t[0] * t[1]
    return per_level, a, b, total


def _hgrn_kernel(q_ref, v_ref, zf_ref, zb_ref, gate_ref, lbf_ref, lbb_ref, ng_ref,
                 x_ref, ya_ref, yb_ref, wout_ref, fg_ref, o_ref,
                 qlv_all, klv_all, abuf_all, sball, sf, sb, ycat_all, *, layer, final):
    ph = pl.program_id(1)
    i = pl.program_id(2)
    n = pl.num_programs(2)
    cs = HGRN_CHUNK
    w = HGRN_WIDTH
    half = HGRN_BLOCK // 2
    g = q_ref.shape[0] // cs
    nb = cs // HGRN_BLOCK
    nl = HGRN_LEVELS
    n_in = 4
    slot_diag, slot_f, slot_b = nl, nl + 1, nl + 2
    mix_c = CONV_WIDTH + SGU_WIDTH
    a_f, b_f, t_f, a_b, b_b, t_b, q32, k_f, k_b = range(9)

    lb_f = _lower_bound_row(lbf_ref, layer)
    lb_b = _lower_bound_row(lbb_ref, layer)
    row = lax.broadcasted_iota(jnp.int32, (half, w), 0)
    heads = [slice(hd * HGRN_DK, (hd + 1) * HGRN_DK) for hd in range(HGRN_HEADS)]

    def halves(x):
        return [x[0:half, :], x[half:, :]]

    def whole(lo_hi):
        return jnp.concatenate(list(lo_hi), axis=0)

    def pair_totals(abuf, planes, hb):
        for blk in range(nb):
            if not (blk & hb):
                r0 = pl.ds(blk * HGRN_BLOCK, HGRN_BLOCK)
                r1 = pl.ds((blk ^ hb) * HGRN_BLOCK, HGRN_BLOCK)
                for plane in planes:
                    prod = abuf[plane, r0, :] * abuf[plane, r1, :]
                    abuf[plane, r0, :] = prod
                    abuf[plane, r1, :] = prod

    def sweep0_chunk(base, c, buf):
        abuf = abuf_all.at[buf]
        sball[c] = sb[...].astype(BF16)

        def scan_block(blk, carry):
            rl = pl.ds(pl.multiple_of(blk * HGRN_BLOCK, HGRN_BLOCK), HGRN_BLOCK)
            rg = pl.ds(pl.multiple_of(base + blk * HGRN_BLOCK, HGRN_BLOCK), HGRN_BLOCK)
            f, k = _decay_and_key(zb_ref[rg, :].astype(F32), lb_b)
            _, _, b, t = _inblock_scan(halves(f), row, forward=False)
            abuf[b_b, rl, :] = whole(b)
            abuf[t_b, rl, :] = whole((t, t))
            abuf[k_b, rl, :] = k
            return carry

        lax.fori_loop(0, nb, scan_block, 0)

        hb = 1
        while hb < nb:
            for blk in range(nb):
                if blk & hb:
                    r = pl.ds(blk * HGRN_BLOCK, HGRN_BLOCK)
                    rp = pl.ds((blk ^ hb) * HGRN_BLOCK, HGRN_BLOCK)
                    abuf[b_b, r, :] = abuf[b_b, r, :] * abuf[t_b, rp, :]
            pair_totals(abuf, (t_b,), hb)
            hb *= 2

        kst = (abuf[k_b] * abuf[b_b]).astype(BF16)
        vb = v_ref[pl.ds(base, cs), :].astype(BF16)
        tot = abuf[t_b, 0:1, :]
        for hd in range(HGRN_HEADS):
            cols = slice(hd * HGRN_DK, (hd + 1) * HGRN_DK)
            upd_s = lax.dot_general(vb[:, cols], kst[:, cols], (((0,), (0,)), ((), ())),
                                    preferred_element_type=F32)
            sb[hd] = sb[hd] * tot[:, cols] + upd_s

    def sweep1_chunk(base, c, buf):
        qlv, klv, abuf, ycat = qlv_all.at[buf], klv_all.at[buf], abuf_all.at[buf], ycat_all.at[buf]
        for blk in range(nb):
            rl = pl.ds(blk * HGRN_BLOCK, HGRN_BLOCK)
            rg = pl.ds(base + blk * HGRN_BLOCK, HGRN_BLOCK)
            q16 = q_ref[rg, :].astype(F32)
            ff, kf = _decay_and_key(zf_ref[rg, :].astype(F32), lb_f)
            fb, kb = _decay_and_key(zb_ref[rg, :].astype(F32), lb_b)
            q, kf2, kb2 = halves(q16), halves(kf), halves(kb)
            lv_f, af, bf_, tf = _inblock_scan(halves(ff), row, forward=True)
            lv_b, ab, bb, tb = _inblock_scan(halves(fb), row, forward=False)
            for lvl in range(n_in - 1):
                bit = (row & (1 << lvl)) != 0
                (fa, fbk), (ba, bbk) = lv_f[lvl], lv_b[lvl]
                qo = [q[s] * jnp.where(bit, fa[s], ba[s]) for s in range(2)]
                ko = [jnp.where(bit, kb2[s] * bbk[s], kf2[s] * fbk[s]) for s in range(2)]
                qlv[lvl, rl, :] = whole(qo).astype(BF16)
                klv[lvl, rl, :] = whole(ko).astype(BF16)
            (fa, fbk), (ba, bbk) = lv_f[n_in - 1], lv_b[n_in - 1]
            qlv[n_in - 1, rl, :] = whole((q[0] * ba[0], q[1] * fa[1])).astype(BF16)
            klv[n_in - 1, rl, :] = whole((kf2[0] * fbk[0], kb2[1] * bbk[1])).astype(BF16)
            qlv[slot_diag, rl, :] = q16.astype(BF16)
            klv[slot_diag, rl, :] = (kf + kb).astype(BF16)
            for plane, val in ((a_f, whole(af)), (b_f, whole(bf_)), (t_f, whole((tf, tf))),
                               (a_b, whole(ab)), (b_b, whole(bb)), (t_b, whole((tb, tb))),
                               (q32, q16), (k_f, kf), (k_b, kb)):
                abuf[plane, rl, :] = val

        hb = 1
        lvl = n_in
        while hb < nb:
            for blk in range(nb):
                r = pl.ds(blk * HGRN_BLOCK, HGRN_BLOCK)
                rp = pl.ds((blk ^ hb) * HGRN_BLOCK, HGRN_BLOCK)
                if blk & hb:
                    qa, kb_ = abuf[a_f, r, :], abuf[b_b, r, :]
                    qlv[lvl, r, :] = (abuf[q32, r, :] * qa).astype(BF16)
                    klv[lvl, r, :] = (abuf[k_b, r, :] * kb_).astype(BF16)
                    abuf[a_f, r, :] = qa * abuf[t_f, rp, :]
                    abuf[b_b, r, :] = kb_ * abuf[t_b, rp, :]
                else:
                    qa, kb_ = abuf[a_b, r, :], abuf[b_f, r, :]
                    qlv[lvl, r, :] = (abuf[q32, r, :] * qa).astype(BF16)
                    klv[lvl, r, :] = (abuf[k_f, r, :] * kb_).astype(BF16)
                    abuf[a_b, r, :] = qa * abuf[t_b, rp, :]
                    abuf[b_f, r, :] = kb_ * abuf[t_f, rp, :]
            pair_totals(abuf, (t_f, t_b), hb)
            hb *= 2
            lvl += 1

        q_all = abuf[q32]
        qlv[slot_f] = (q_all * abuf[a_f]).astype(BF16)
        qlv[slot_b] = (q_all * abuf[a_b]).astype(BF16)
        klv[slot_f] = (abuf[k_f] * abuf[b_f]).astype(BF16)
        tot_f = abuf[t_f, 0:1, :]

        ti = lax.broadcasted_iota(jnp.int32, (cs, cs), 0)
        si = lax.broadcasted_iota(jnp.int32, (cs, cs), 1)
        x = ti ^ si
        level_id = jnp.where(x == 0, slot_diag, 31 - lax.clz(x))

        rows = pl.ds(base, cs)
        vb = v_ref[rows, :].astype(BF16)
        ng = ng_ref[...]
        nt_dims = (((1,), (1,)), ((), ()))
        scores = [None] * HGRN_HEADS
        for lvl in range(nl + 1):
            mask = level_id == lvl
            for hd, cols in enumerate(heads):
                s_l = lax.dot_general(qlv[lvl, :, cols], klv[lvl, :, cols], nt_dims,
                                      preferred_element_type=F32).astype(BF16)
                scores[hd] = s_l if lvl == 0 else jnp.where(mask, s_l, scores[hd])
        for hd, cols in enumerate(heads):
            o = jnp.dot(scores[hd], vb[:, cols], preferred_element_type=F32)
            q_in = jnp.concatenate([qlv[slot_f, :, cols], qlv[slot_b, :, cols]], axis=1)
            s_in = jnp.concatenate([sf[hd].astype(BF16), sball[c, hd]], axis=1)
            o = o + lax.dot_general(q_in, s_in, nt_dims, preferred_element_type=F32)
            o = o * lax.rsqrt(jnp.mean(o * o, axis=-1, keepdims=True) + EPS) * ng
            ycat[:, mix_c + hd * HGRN_DK:mix_c + (hd + 1) * HGRN_DK] = (
                o * jax.nn.silu(gate_ref[rows, cols].astype(F32))).astype(BF16)
            upd_s = lax.dot_general(vb[:, cols], klv[slot_f, :, cols], (((0,), (0,)), ((), ())),
                                    preferred_element_type=F32)
            sf[hd] = sf[hd] * tot_f[:, cols] + upd_s

        ycat[:, 0:CONV_WIDTH] = ya_ref[rows, :]
        ycat[:, CONV_WIDTH:mix_c] = yb_ref[rows, :]
        acc = x_ref[rows, :] + jnp.dot(ycat[...], wout_ref[...], preferred_element_type=F32)
        if final:
            ms = jnp.mean(acc * acc, axis=-1, keepdims=True)
            acc = acc * lax.rsqrt(ms + EPS) * fg_ref[...]
        o_ref[rows, :] = acc

    @pl.when(ph == 0)
    def _():
        @pl.when(i == 0)
        def _():
            sb[...] = jnp.zeros_like(sb)

        for j in reversed(range(g)):
            sweep0_chunk(j * cs, (n - 1 - i) * g + j, j % 2)

    @pl.when(ph == 1)
    def _():
        @pl.when(i == 0)
        def _():
            sf[...] = jnp.zeros_like(sf)

        for j in range(g):
            sweep1_chunk(j * cs, i * g + j, j % 2)


def _hgrn_outproj(p, lb_fwd, lb_bwd, norm_g, x, ya, yb, w_out_bf16, final_g, layer, final):
    b, s, _ = p.shape
    cs = HGRN_CHUNK
    rows = HGRN_CHUNKS_PER_STEP * cs
    n = s // rows
    cq, ci, cff, cfb, cg = HGRN_COLS

    def both(col):
        return pl.BlockSpec((None, rows, HGRN_WIDTH),
                            lambda bi, ph, i: (bi, jnp.where(ph == 0, n - 1 - i, i), col))

    def second(col, width=HGRN_WIDTH):
        return pl.BlockSpec((None, rows, width), lambda bi, ph, i: (bi, ph * i, col))

    def whole(shape):
        return pl.BlockSpec(shape, lambda bi, ph, i: (0,) * len(shape))

    n_slots = HGRN_LEVELS + 3
    return pl.pallas_call(
        functools.partial(_hgrn_kernel, layer=layer, final=final),
        grid=(b, 2, n),
        in_specs=[second(cq), both(ci), second(cff), both(cfb), second(cg),
                  whole((DEPTH, HGRN_WIDTH)), whole((DEPTH, HGRN_WIDTH)), whole((1, HGRN_DK)),
                  second(0, D_MODEL), second(0, CONV_WIDTH), second(0, SGU_WIDTH),
                  whole((D_MIX, D_MODEL)), whole((1, D_MODEL))],
        out_specs=second(0, D_MODEL),
        out_shape=jax.ShapeDtypeStruct((b, s, D_MODEL), F32),
        scratch_shapes=[
            pltpu.VMEM((2, n_slots, cs, HGRN_WIDTH), BF16),
            pltpu.VMEM((2, n_slots, cs, HGRN_WIDTH), BF16),
            pltpu.VMEM((2, 9, cs, HGRN_WIDTH), F32),
            pltpu.VMEM((s // cs, HGRN_HEADS, HGRN_DK, HGRN_DK), BF16),
            pltpu.VMEM((HGRN_HEADS, HGRN_DK, HGRN_DK), F32),
            pltpu.VMEM((HGRN_HEADS, HGRN_DK, HGRN_DK), F32),
            pltpu.VMEM((2, cs, D_MIX), BF16),
        ],
        compiler_params=_params(("arbitrary", "arbitrary", "arbitrary")),
        name="hgrn_branch",
    )(p, p, p, p, p, lb_fwd, lb_bwd, norm_g, x, ya, yb, w_out_bf16, final_g)


def _trunk(x, prm):
    for l in range(DEPTH):
        p, ya, yb = _front(x, prm["norm_g"][l], prm["w_in"][l], prm["conv_w"][l], prm["conv_b"][l],
                           prm["conv_ln_g"][l], prm["conv_ln_b"][l], prm["conv_pw"][l],
                           prm["sgu_ln_g"][l], prm["sgu_ln_b"][l], prm["sgu_w"][l], prm["sgu_b"][l])
        x = _hgrn_outproj(p, prm["hgrn_lb_fwd"], prm["hgrn_lb_bwd"], prm["hgrn_norm_g"][l], x, ya, yb,
                          prm["w_out"][l], prm["final_norm_g"], l, final=(l == DEPTH - 1))
    return x


def kernel(x_prompt, x_sample, norm_g, w_in, conv_w, conv_b, conv_ln_g, conv_ln_b, conv_pw,
           sgu_ln_g, sgu_ln_b, sgu_w, sgu_b, hgrn_lb_fwd, hgrn_lb_bwd, hgrn_norm_g,
           w_out, final_norm_g):
    row = lambda a: a[:, None, :]
    prm = {
        "norm_g": row(norm_g),
        "w_in": w_in.astype(BF16),
        "conv_w": conv_w,
        "conv_b": row(conv_b),
        "conv_ln_g": row(conv_ln_g),
        "conv_ln_b": row(conv_ln_b),
        "conv_pw": conv_pw.astype(BF16),
        "sgu_ln_g": row(sgu_ln_g),
        "sgu_ln_b": row(sgu_ln_b),
        "sgu_w": sgu_w.transpose(0, 2, 1, 3).reshape(DEPTH, SGU_CHUNK, SGU_HEADS * SGU_CHUNK).astype(BF16),
        "sgu_b": jnp.repeat(sgu_b.transpose(0, 2, 1), SGU_HEAD_DIM, axis=2),
        "hgrn_lb_fwd": hgrn_lb_fwd,
        "hgrn_lb_bwd": hgrn_lb_bwd,
        "hgrn_norm_g": row(hgrn_norm_g),
        "w_out": w_out.astype(BF16),
        "final_norm_g": final_norm_g[None, :],
    }
    return (_trunk(x_prompt, prm), _trunk(x_sample, prm))
```
